```python
import math
import jax, jax.numpy as jnp
from jax import lax
import numpy as np

D_MODEL = 1024
BATCH = 8
SEQ = 4096
DEPTH = 4

N_MIXERS = 2
D_PLE = 256
D_FF = 2816
RMS_EPS = 1e-6

SSD_EXPAND = 2
SSD_D_INNER = SSD_EXPAND * D_MODEL
SSD_HEAD_DIM = 64
SSD_HEADS = SSD_D_INNER // SSD_HEAD_DIM
SSD_GROUPS = 8
SSD_D_STATE = 128
SSD_CONV = 5
SSD_CHUNK = 128
SSD_BC = SSD_GROUPS * SSD_D_STATE
SSD_CONV_DIM = SSD_D_INNER + 4 * SSD_BC
SSD_IN_DIM = SSD_D_INNER + SSD_CONV_DIM + 2 * SSD_HEADS

GRID_W = 64
NA_HEAD_DIM = 64
NA_HEADS = D_MODEL // NA_HEAD_DIM
NA_WIN_ROWS = 8
NA_WIN_COLS = 16
NA_QBLOCK_COLS = 16
NA_KBLOCK_COLS = NA_QBLOCK_COLS + NA_WIN_COLS
NA_RPB_ROWS = 2 * NA_WIN_ROWS - 1
NA_RPB_COLS = 2 * NA_WIN_COLS - 1

N_SSD_LAYERS = (DEPTH + 1) // 2
N_NA_LAYERS = DEPTH // 2

kernel_name = "bidir_hybrid_ssd_natten_macaron"


def rmsnorm(x, g):
    xf = x.astype(jnp.float32)
    y = xf * lax.rsqrt(jnp.mean(xf * xf, axis=-1, keepdims=True) + RMS_EPS)
    return (y * g.astype(jnp.float32)).astype(x.dtype)


def swiglu(x, w_gu, w_down):
    g, u = jnp.split(x @ w_gu, 2, axis=-1)
    return (jax.nn.silu(g) * u) @ w_down


def centred_depthwise_conv(x, w, b):
    k, c = w.shape
    y = lax.conv_general_dilated(x, w[:, None, :].astype(x.dtype), window_strides=(1,),
                                 padding=((k // 2, k // 2),),
                                 dimension_numbers=('NWC', 'WIO', 'NWC'),
                                 feature_group_count=c)
    return y + b


def ssd_scan(x, dt, a, b_mat, c_mat):
    bsz, seqlen, nh, hp = x.shape
    ng, ns = b_mat.shape[-2:]
    hg = nh // ng
    q = SSD_CHUNK
    nc = seqlen // q
    xc = x.reshape(bsz, nc, q, ng, hg, hp)
    bc = b_mat.reshape(bsz, nc, q, ng, ns)
    cc = c_mat.reshape(bsz, nc, q, ng, ns)
    dtc = dt.astype(jnp.float32).reshape(bsz, nc, q, ng, hg)
    cum = jnp.cumsum(dtc * a.astype(jnp.float32).reshape(ng, hg), axis=2)
    xdt = xc * dtc[..., None].astype(x.dtype)
    lower = jnp.tril(jnp.ones((q, q), dtype=bool))[:, :, None, None]
    seg = cum[:, :, :, None] - cum[:, :, None, :]
    cb = jnp.einsum('bcqgn,bckgn->bcqkg', cc, bc)
    scores = (cb[..., None] * jnp.exp(jnp.where(lower, seg, -jnp.inf))).astype(x.dtype)
    y_diag = jnp.einsum('bcqkgj,bckgjp->bcqgjp', scores, xdt)
    decay_to_end = jnp.exp(cum[:, :, -1:] - cum)
    states = jnp.einsum('bckgn,bckgjp->bcgjpn', bc, xdt * decay_to_end[..., None].astype(x.dtype))
    chunk_decay = jnp.exp(cum[:, :, -1]).astype(x.dtype)

    def step(carry, inp):
        st, dec = inp
        return carry * dec[..., None, None] + st, carry

    _, prev = lax.scan(step, jnp.zeros_like(states[:, 0]),
                       (jnp.moveaxis(states, 1, 0), jnp.moveaxis(chunk_decay, 1, 0)))
    prev = jnp.moveaxis(prev, 0, 1)
    y_off = jnp.einsum('bcqgn,bcgjpn->bcqgjp', cc, prev) * jnp.exp(cum)[..., None].astype(x.dtype)
    return (y_diag + y_off).reshape(bsz, seqlen, nh, hp)


def ssd_mixer(h, w_in, conv_w, conv_b, dt_bias, a_log, d_skip, norm_g, w_out):
    bsz, seqlen, _ = h.shape
    proj = h @ w_in
    z, xbc, dt_raw = jnp.split(proj, [SSD_D_INNER, SSD_D_INNER + SSD_CONV_DIM], axis=-1)
    xbc = jax.nn.silu(centred_depthwise_conv(xbc, conv_w, conv_b))
    xs, b_f, c_f, b_b, c_b = jnp.split(
        xbc, [SSD_D_INNER, SSD_D_INNER + SSD_BC, SSD_D_INNER + 2 * SSD_BC, SSD_D_INNER + 3 * SSD_BC], axis=-1)
    xs = xs.reshape(bsz, seqlen, SSD_HEADS, SSD_HEAD_DIM)
    grp = lambda t: t.reshape(bsz, seqlen, SSD_GROUPS, SSD_D_STATE)
    flip = lambda t: jnp.flip(t, axis=1)
    dt = jax.nn.softplus(dt_raw.reshape(bsz, seqlen, 2, SSD_HEADS) + dt_bias)
    a = -jnp.exp(a_log.astype(jnp.float32))
    y_fwd = ssd_scan(xs, dt[:, :, 0], a[0], grp(b_f), grp(c_f))
    y_bwd = flip(ssd_scan(flip(xs), flip(dt[:, :, 1]), a[1], flip(grp(b_b)), flip(grp(c_b))))
    y = (y_fwd + y_bwd + xs * d_skip[:, None]).reshape(bsz, seqlen, SSD_D_INNER)
    y = rmsnorm(y * jax.nn.silu(z), norm_g)
    return y @ w_out


def neighbourhood_attention(h, w_qkv, q_norm, k_norm, rpb, w_out):
    bsz, seqlen, _ = h.shape
    rows = seqlen // GRID_W
    win_r = min(NA_WIN_ROWS, rows)
    n_cb = GRID_W // NA_QBLOCK_COLS
    qkv = (h @ w_qkv).reshape(bsz, rows, GRID_W, 3, NA_HEADS, NA_HEAD_DIM)
    q = rmsnorm(qkv[:, :, :, 0], q_norm) * (NA_HEAD_DIM ** -0.5)
    k = rmsnorm(qkv[:, :, :, 1], k_norm)
    v = qkv[:, :, :, 2]
    q_col = np.arange(GRID_W).reshape(n_cb, NA_QBLOCK_COLS)
    win_c0 = np.clip(q_col - NA_WIN_COLS // 2, 0, GRID_W - NA_WIN_COLS)
    kblk_c0 = np.clip(q_col[:, 0] - NA_WIN_COLS // 2, 0, GRID_W - NA_KBLOCK_COLS)
    k_col = kblk_c0[:, None] + np.arange(NA_KBLOCK_COLS)
    kc_b = k_col[:, None, :]
    col_valid = jnp.asarray((kc_b >= win_c0[..., None]) & (kc_b < win_c0[..., None] + NA_WIN_COLS))
    col_rel = jnp.asarray(np.clip(kc_b - q_col[..., None] + NA_WIN_COLS - 1, 0, NA_RPB_COLS - 1))

    def row_block(args):
        r, q_row = args
        r0 = jnp.clip(r - win_r // 2, 0, rows - win_r)
        k_blk = lax.dynamic_slice_in_dim(k, r0, win_r, axis=1)[:, :, k_col]
        v_blk = lax.dynamic_slice_in_dim(v, r0, win_r, axis=1)[:, :, k_col]
        q_blk = q_row.reshape(bsz, n_cb, NA_QBLOCK_COLS, NA_HEADS, NA_HEAD_DIM)
        s = jnp.einsum('bxqhd,brxkhd->bhxqrk', q_blk, k_blk).astype(jnp.float32)
        row_rel = r0 - r + jnp.arange(win_r) + NA_WIN_ROWS - 1
        bias = jnp.transpose(rpb[:, row_rel][:, :, col_rel], (0, 2, 3, 1, 4))
        bias = jnp.where(col_valid[:, :, None, :], bias.astype(jnp.float32), -jnp.inf)
        s = s + bias
        pr = jax.nn.softmax(s.reshape(s.shape[:4] + (-1,)), axis=-1).reshape(s.shape).astype(v.dtype)
        o = jnp.einsum('bhxqrk,brxkhd->bxqhd', pr, v_blk)
        return o.reshape(bsz, GRID_W, NA_HEADS * NA_HEAD_DIM)

    out = lax.map(row_block, (jnp.arange(rows), jnp.moveaxis(q, 1, 0)))
    out = jnp.moveaxis(out, 0, 1).reshape(bsz, seqlen, NA_HEADS * NA_HEAD_DIM)
    return out @ w_out


def setup_inputs(seed: int = 0) -> dict:
    key = jax.random.key(seed)
    keys = list(jax.random.split(key, 32))
    nxt = lambda: keys.pop()
    nrm = lambda shape, scale: jax.random.normal(nxt(), shape, jnp.float32) * scale
    gain = lambda shape: 1.0 + nrm(shape, 0.02)
    ns, nn_ = N_SSD_LAYERS, N_NA_LAYERS
    dt0 = jnp.exp(jax.random.uniform(nxt(), (ns, 2, SSD_HEADS), jnp.float32,
                                     minval=math.log(1e-3), maxval=math.log(1e-1)))
    dt_bias = dt0 + jnp.log(-jnp.expm1(-dt0))
    a_log = jnp.log(jax.random.uniform(nxt(), (ns, 2, SSD_HEADS), jnp.float32, minval=1.0, maxval=16.0))
    return {
        "x": nrm((BATCH, SEQ, D_MODEL), 1.0),
        "p": nrm((DEPTH, BATCH, SEQ, D_PLE), 1.0),
        "ffn1_norm": gain((DEPTH, D_MODEL)),
        "ffn1_w_gu": nrm((DEPTH, D_MODEL, 2 * D_FF), D_MODEL ** -0.5),
        "ffn1_w_down": nrm((DEPTH, D_FF, D_MODEL), D_FF ** -0.5),
        "mix_norm": gain((DEPTH, D_MODEL)),
        "ffn2_norm": gain((DEPTH, D_MODEL)),
        "ffn2_w_gu": nrm((DEPTH, D_MODEL, 2 * D_FF), D_MODEL ** -0.5),
        "ffn2_w_down": nrm((DEPTH, D_FF, D_MODEL), D_FF ** -0.5),
        "ple_norm": gain((DEPTH, D_MODEL)),
        "ple_w_gate": nrm((DEPTH, D_MODEL, D_MODEL), D_MODEL ** -0.5),
        "ple_w_proj": nrm((DEPTH, D_PLE, D_MODEL), D_PLE ** -0.5),
        "ple_post_norm": gain((DEPTH, D_MODEL)),
        "ssd_w_in": nrm((ns, D_MODEL, SSD_IN_DIM), D_MODEL ** -0.5),
        "ssd_conv_w": nrm((ns, SSD_CONV, SSD_CONV_DIM), SSD_CONV ** -0.5),
        "ssd_conv_b": nrm((ns, SSD_CONV_DIM), 0.02),
        "ssd_dt_bias": dt_bias,
        "ssd_a_log": a_log,
        "ssd_d": gain((ns, SSD_HEADS)),
        "ssd_norm": gain((ns, SSD_D_INNER)),
        "ssd_w_out": nrm((ns, SSD_D_INNER, D_MODEL), SSD_D_INNER ** -0.5),
        "na_w_qkv": nrm((nn_, D_MODEL, 3 * NA_HEADS * NA_HEAD_DIM), D_MODEL ** -0.5),
        "na_q_norm": gain((nn_, NA_HEAD_DIM)),
        "na_k_norm": gain((nn_, NA_HEAD_DIM)),
        "na_rpb": nrm((nn_, NA_HEADS, NA_RPB_ROWS, NA_RPB_COLS), 0.02),
        "na_w_out": nrm((nn_, NA_HEADS * NA_HEAD_DIM, D_MODEL), (NA_HEADS * NA_HEAD_DIM) ** -0.5),
    }


def reference(x, p, ffn1_norm, ffn1_w_gu, ffn1_w_down, mix_norm, ffn2_norm, ffn2_w_gu, ffn2_w_down,
              ple_norm, ple_w_gate, ple_w_proj, ple_post_norm,
              ssd_w_in, ssd_conv_w, ssd_conv_b, ssd_dt_bias, ssd_a_log, ssd_d, ssd_norm, ssd_w_out,
              na_w_qkv, na_q_norm, na_k_norm, na_rpb, na_w_out):
    h = x
    for i in range(DEPTH):
        h = h + 0.5 * swiglu(rmsnorm(h, ffn1_norm[i]), ffn1_w_gu[i], ffn1_w_down[i])
        hn = rmsnorm(h, mix_norm[i])
        j = i // N_MIXERS
        if i % N_MIXERS == 0:
            h = h + ssd_mixer(hn, ssd_w_in[j], ssd_conv_w[j], ssd_conv_b[j], ssd_dt_bias[j],
                              ssd_a_log[j], ssd_d[j], ssd_norm[j], ssd_w_out[j])
        else:
            h = h + neighbourhood_attention(hn, na_w_qkv[j], na_q_norm[j], na_k_norm[j],
                                            na_rpb[j], na_w_out[j])
        h = h + 0.5 * swiglu(rmsnorm(h, ffn2_norm[i]), ffn2_w_gu[i], ffn2_w_down[i])
        gate = jax.nn.sigmoid(rmsnorm(h, ple_norm[i]) @ ple_w_gate[i])
        h = h + gate * rmsnorm(p[i] @ ple_w_proj[i], ple_post_norm[i])
    return h
```

```python
import functools
import math

import jax
import jax.numpy as jnp
import numpy as np
from jax import lax
from jax.experimental import pallas as pl
from jax.experimental.pallas import tpu as pltpu

F32 = jnp.float32
BF16 = jnp.bfloat16

RMS_EPS = 1e-6
SSD_HEAD_DIM = 64
SSD_HEADS = 32
SSD_GROUPS = 8
SSD_D_STATE = 128
SSD_CONV = 5
SSD_CHUNK = 128
GRID_W = 64
NA_HEAD_DIM = 64
NA_WIN_ROWS = 8
NA_WIN_COLS = 16
FFN_CHUNK = 256
CONV_HALO = 8
VMEM_LIMIT = 56 * 1024 * 1024
MASK_VALUE = -1e30


def _params(*sem):
    return pltpu.CompilerParams(dimension_semantics=sem, vmem_limit_bytes=VMEM_LIMIT)


def _const_spec(shape):
    nd = len(shape)
    return pl.BlockSpec(shape, lambda *_: (0,) * nd, pipeline_mode=pl.Buffered(1))


def _rms(x, g):
    ms = jnp.mean(x * x, axis=-1, keepdims=True)
    return x * lax.rsqrt(ms + RMS_EPS) * g


def _sigmoid(x):
    return 1.0 / (1.0 + jnp.exp(-x))


def _dot(a, b):
    return jnp.dot(a, b, preferred_element_type=F32)


def _split2(x):
    hi = x.astype(BF16)
    lo = (x - hi.astype(F32)).astype(BF16)
    return hi, lo


def _split3(x):
    hi = x.astype(BF16)
    r = x - hi.astype(F32)
    mid = r.astype(BF16)
    lo = (r - mid.astype(F32)).astype(BF16)
    return hi, mid, lo


def _ffn_body(h_ref, g_ref, wgu_ref, wd_ref, a_ref):
    x = h_ref[...]
    xn = _rms(x, g_ref[...]).astype(BF16)
    n_chunks = wgu_ref.shape[0]
    for c in range(n_chunks):
        gu = _dot(xn, wgu_ref[c])
        gate = gu[:, :FFN_CHUNK]
        up = gu[:, FFN_CHUNK:]
        a_ref[:, c * FFN_CHUNK:(c + 1) * FFN_CHUNK] = (gate * _sigmoid(gate) * up).astype(BF16)
    return x + 0.5 * _dot(a_ref[...], wd_ref[...])


def _ffn_kernel(h_ref, g_ref, wgu_ref, wd_ref, o_ref, a_ref):
    o_ref[...] = _ffn_body(h_ref, g_ref, wgu_ref, wd_ref, a_ref)


def _ffn_ple_kernel(h_ref, g_ref, wgu_ref, wd_ref, p_ref, pg_ref, wgate_ref, wproj_ref, ppg_ref, o_ref, a_ref):
    h2 = _ffn_body(h_ref, g_ref, wgu_ref, wd_ref, a_ref)
    gate = _sigmoid(_dot(_rms(h2, pg_ref[...]).astype(BF16), wgate_ref[...]))
    emb = _dot(p_ref[...].astype(BF16), wproj_ref[...])
    o_ref[...] = h2 + gate * _rms(emb, ppg_ref[...])


def _prep_ffn_weights(w_gu, w_down):
    d_model, two_ff = w_gu.shape
    d_ff = two_ff // 2
    n_chunks = d_ff // FFN_CHUNK
    wg = w_gu[:, :d_ff].reshape(d_model, n_chunks, FFN_CHUNK)
    wu = w_gu[:, d_ff:].reshape(d_model, n_chunks, FFN_CHUNK)
    wgu = jnp.concatenate([wg, wu], axis=-1).transpose(1, 0, 2).astype(BF16)
    return wgu, w_down.astype(BF16)


def _ffn(h, norm_g, w_gu, w_down, ple=None, tm=512):
    t, d = h.shape
    wgu, wd = _prep_ffn_weights(w_gu, w_down)
    d_ff = wd.shape[0]
    row = lambda i: (i, 0)
    in_specs = [pl.BlockSpec((tm, d), row), _const_spec((1, d)), _const_spec(wgu.shape), _const_spec(wd.shape)]
    args = [h, norm_g.reshape(1, d), wgu, wd]
    kern = _ffn_kernel
    if ple is not None:
        p, ple_g, w_gate, w_proj, post_g = ple
        dp = p.shape[-1]
        in_specs += [pl.BlockSpec((tm, dp), row), _const_spec((1, d)), _const_spec((d, d)),
                     _const_spec((dp, d)), _const_spec((1, d))]
        args += [p, ple_g.reshape(1, d), w_gate.astype(BF16), w_proj.astype(BF16), post_g.reshape(1, d)]
        kern = _ffn_ple_kernel
    return pl.pallas_call(
        kern,
        grid=(t // tm,),
        in_specs=in_specs,
        out_specs=pl.BlockSpec((tm, d), row),
        out_shape=jax.ShapeDtypeStruct((t, d), F32),
        scratch_shapes=[pltpu.VMEM((tm, d_ff), BF16)],
        compiler_params=_params("parallel"),
        name="ffn_ple" if ple is not None else "ffn",
    )(*args)


def _ssd_in_kernel(h_ref, g_ref, wz_ref, wx_ref, wdt_ref, dtb_ref, z_ref, xbc_ref, dt_ref):
    xn = _rms(h_ref[...], g_ref[...]).astype(BF16)
    z_ref[...] = _dot(xn, wz_ref[...])
    n_chunks = wx_ref.shape[0]
    cw = wx_ref.shape[2]
    for c in range(n_chunks):
        xbc_ref[:, c * cw:(c + 1) * cw] = _dot(xn, wx_ref[c])
    raw = _dot(xn, wdt_ref[...]) + dtb_ref[...]
    dt_ref[...] = jnp.maximum(raw, 0.0) + jnp.log1p(jnp.exp(-jnp.abs(raw)))


def _ssd_in(h, norm_g, w_in, dt_bias, d_inner, conv_dim, tm=256, cw=512):
    t, d = h.shape
    n_dt = w_in.shape[1] - d_inner - conv_dim
    wz = w_in[:, :d_inner].astype(BF16)
    wx = w_in[:, d_inner:d_inner + conv_dim].reshape(d, conv_dim // cw, cw).transpose(1, 0, 2).astype(BF16)
    wdt = jnp.pad(w_in[:, d_inner + conv_dim:], ((0, 0), (0, 128 - n_dt))).astype(BF16)
    dtb = jnp.pad(dt_bias.reshape(1, n_dt), ((0, 0), (0, 128 - n_dt)))
    row = lambda i: (i, 0)
    return pl.pallas_call(
        _ssd_in_kernel,
        grid=(t // tm,),
        in_specs=[pl.BlockSpec((tm, d), row), _const_spec((1, d)), _const_spec(wz.shape), _const_spec(wx.shape),
                  _const_spec(wdt.shape), _const_spec((1, 128))],
        out_specs=[pl.BlockSpec((tm, d_inner), row), pl.BlockSpec((tm, conv_dim), row), pl.BlockSpec((tm, 128), row)],
        out_shape=[jax.ShapeDtypeStruct((t, d_inner), F32), jax.ShapeDtypeStruct((t, conv_dim), F32),
                   jax.ShapeDtypeStruct((t, 128), F32)],
        compiler_params=_params("parallel"),
        name="ssd_in",
    )(h, norm_g.reshape(1, d), wz, wx, wdt, dtb)


def _conv_kernel(x_ref, prev_ref, next_ref, w_ref, b_ref, o_ref, ext_ref):
    i = pl.program_id(1)
    n = pl.num_programs(1)
    tl = x_ref.shape[1]
    half = SSD_CONV // 2
    ext_ref[0:CONV_HALO, :] = jnp.where(i > 0, prev_ref[0], 0.0)
    ext_ref[CONV_HALO:CONV_HALO + tl, :] = x_ref[0]
    ext_ref[CONV_HALO + tl:, :] = jnp.where(i < n - 1, next_ref[0], 0.0)
    acc = b_ref[...] + w_ref[0:1, :] * ext_ref[CONV_HALO - half:CONV_HALO - half + tl, :]
    for j in range(1, SSD_CONV):
        acc = acc + w_ref[j:j + 1, :] * ext_ref[CONV_HALO - half + j:CONV_HALO - half + j + tl, :]
    o_ref[0] = (acc * _sigmoid(acc)).astype(o_ref.dtype)


def _conv_silu(xbc, conv_w, conv_b, ch0, n_ch, out_dtype, tl=512, tc=512):
    bsz, seqlen, _ = xbc.shape
    c0 = ch0 // tc
    hb = tl // CONV_HALO
    n_hblk = seqlen // CONV_HALO
    w = jnp.pad(conv_w[:, ch0:ch0 + n_ch], ((0, 8 - SSD_CONV), (0, 0)))
    b = conv_b[ch0:ch0 + n_ch].reshape(1, n_ch)
    return pl.pallas_call(
        _conv_kernel,
        grid=(bsz, seqlen // tl, n_ch // tc),
        in_specs=[
            pl.BlockSpec((1, tl, tc), lambda bi, i, c: (bi, i, c + c0)),
            pl.BlockSpec((1, CONV_HALO, tc), lambda bi, i, c: (bi, jnp.maximum(i * hb - 1, 0), c + c0)),
            pl.BlockSpec((1, CONV_HALO, tc), lambda bi, i, c: (bi, jnp.minimum((i + 1) * hb, n_hblk - 1), c + c0)),
            pl.BlockSpec((8, tc), lambda bi, i, c: (0, c)),
            pl.BlockSpec((1, tc), lambda bi, i, c: (0, c)),
        ],
        out_specs=pl.BlockSpec((1, tl, tc), lambda bi, i, c: (bi, i, c)),
        out_shape=jax.ShapeDtypeStruct((bsz, seqlen, n_ch), out_dtype),
        scratch_shapes=[pltpu.VMEM((tl + 2 * CONV_HALO, tc), F32)],
        compiler_params=_params("parallel", "parallel", "parallel"),
        name="ssd_conv",
    )(xbc, xbc, xbc, w, b)


def _ssd_scan_kernel(xs_ref, b_ref, c_ref, dt_ref, a_ref, tri_ref, e_ref, extra_ref, dsk_ref, y_ref, st_ref,
                     *, reverse, col0):
    q = SSD_CHUNK
    hg = SSD_HEADS // SSD_GROUPS
    gw = hg * SSD_HEAD_DIM

    @pl.when(pl.program_id(1) == 0)
    def _():
        st_ref[...] = jnp.zeros_like(st_ref)

    dt = dt_ref[0]
    dta = dt * a_ref[...]
    hi, mid, lo = _split3(dta)
    tri = tri_ref[...]
    cum = _dot(tri, hi) + _dot(tri, mid) + _dot(tri, lo)
    cum_t = cum.T
    ecum = jnp.exp(cum)
    edge = cum[q - 1:q, :] if not reverse else cum[0:1, :]
    dt_dec = dt * jnp.exp(edge - cum)

    parts = _split2(dt) + _split2(dt_dec) + _split2(ecum)
    stacked = jnp.concatenate(parts, axis=0)
    ex = _dot(stacked, e_ref[...])
    dt_x = ex[0:q] + ex[q:2 * q]
    dd_x = ex[2 * q:3 * q] + ex[3 * q:4 * q]
    ec_x = ex[4 * q:5 * q] + ex[5 * q:6 * q]
    edge_x = ec_x[q - 1:q, :] if not reverse else ec_x[0:1, :]

    xs = xs_ref[0]
    xdt = (xs * dt_x).astype(BF16)
    xdd = (xs * dd_x).astype(BF16)

    rows = lax.broadcasted_iota(jnp.int32, (q, q), 0)
    cols = lax.broadcasted_iota(jnp.int32, (q, q), 1)
    causal = (rows >= cols) if not reverse else (rows <= cols)
    lane_head = lax.broadcasted_iota(jnp.int32, (q, gw), 1) // SSD_HEAD_DIM

    for g in range(SSD_GROUPS):
        bg = b_ref[0, :, g * SSD_D_STATE:(g + 1) * SSD_D_STATE]
        cg = c_ref[0, :, g * SSD_D_STATE:(g + 1) * SSD_D_STATE]
        cb = lax.dot_general(cg, bg, (((1,), (1,)), ((), ())), preferred_element_type=F32)
        xdt_g = xdt[:, g * gw:(g + 1) * gw]
        scores, blocks = [], []
        for j in range(hg):
            col = col0 + g * hg + j
            seg = cum[:, col:col + 1] - cum_t[col:col + 1, :]
            decay = jnp.exp(jnp.where(causal, seg, -jnp.inf))
            scores.append((cb * decay).astype(BF16))
            blocks.append(jnp.where(lane_head == j, xdt_g, jnp.zeros_like(xdt_g)))
        y_diag = _dot(jnp.concatenate(scores, axis=1), jnp.concatenate(blocks, axis=0))
        st = st_ref[g]
        y_off = _dot(cg, st.astype(BF16)) * ec_x[:, g * gw:(g + 1) * gw]
        y = y_diag + y_off
        if extra_ref is not None:
            y = y + extra_ref[0, :, g * gw:(g + 1) * gw]
        if dsk_ref is not None:
            y = y + xs[:, g * gw:(g + 1) * gw] * dsk_ref[:, g * gw:(g + 1) * gw]
        y_ref[0, :, g * gw:(g + 1) * gw] = y
        new = lax.dot_general(bg, xdd[:, g * gw:(g + 1) * gw], (((0,), (0,)), ((), ())),
                              preferred_element_type=F32)
        st_ref[g] = st * edge_x[:, g * gw:(g + 1) * gw] + new


def _scan_fwd_kernel(xs_ref, b_ref, c_ref, dt_ref, a_ref, tri_ref, e_ref, dsk_ref, y_ref, st_ref):
    _ssd_scan_kernel(xs_ref, b_ref, c_ref, dt_ref, a_ref, tri_ref, e_ref, None, dsk_ref, y_ref, st_ref,
                     reverse=False, col0=0)


def _scan_bwd_kernel(xs_ref, b_ref, c_ref, dt_ref, a_ref, tri_ref, e_ref, yf_ref, y_ref, st_ref):
    _ssd_scan_kernel(xs_ref, b_ref, c_ref, dt_ref, a_ref, tri_ref, e_ref, yf_ref, None, y_ref, st_ref,
                     reverse=True, col0=SSD_HEADS)


def _ssd_scan(xs, bc, dt, a_log, d_skip):
    bsz, seqlen, d_inner = xs.shape
    q = SSD_CHUNK
    nc = seqlen // q
    bcw = SSD_GROUPS * SSD_D_STATE
    a = jnp.pad(-jnp.exp(a_log.astype(F32)).reshape(1, 2 * SSD_HEADS), ((0, 0), (0, 128 - 2 * SSD_HEADS)))
    dsk = jnp.repeat(d_skip.astype(F32), SSD_HEAD_DIM).reshape(1, d_inner)
    lower = np.tril(np.ones((q, q), np.float32))
    chan_head = np.arange(d_inner) // SSD_HEAD_DIM
    st_shape = pltpu.VMEM((SSD_GROUPS, SSD_D_STATE, d_inner // SSD_GROUPS), F32)

    def expand(col0):
        e = (np.arange(128)[:, None] == (chan_head[None, :] + col0)).astype(np.float32)
        return jnp.asarray(e, BF16)

    def specs(cidx, b_blk, c_blk):
        return [
            pl.BlockSpec((1, q, d_inner), lambda bi, c: (bi, cidx(c), 0)),
            pl.BlockSpec((1, q, bcw), lambda bi, c: (bi, cidx(c), b_blk)),
            pl.BlockSpec((1, q, bcw), lambda bi, c: (bi, cidx(c), c_blk)),
            pl.BlockSpec((1, q, 128), lambda bi, c: (bi, cidx(c), 0)),
            _const_spec((1, 128)), _const_spec((q, q)), _const_spec((128, d_inner)),
        ]

    fwd = lambda c: c
    y_f = pl.pallas_call(
        _scan_fwd_kernel,
        grid=(bsz, nc),
        in_specs=specs(fwd, 0, 1) + [_const_spec((1, d_inner))],
        out_specs=pl.BlockSpec((1, q, d_inner), lambda bi, c: (bi, c, 0)),
        out_shape=jax.ShapeDtypeStruct((bsz, seqlen, d_inner), F32),
        scratch_shapes=[st_shape],
        compiler_params=_params("parallel", "arbitrary"),
        name="ssd_scan_fwd",
    )(xs, bc, bc, dt, a, jnp.asarray(lower, BF16), expand(0), dsk)
    bwd = lambda c: nc - 1 - c
    return pl.pallas_call(
        _scan_bwd_kernel,
        grid=(bsz, nc),
        in_specs=specs(bwd, 2, 3) + [pl.BlockSpec((1, q, d_inner), lambda bi, c: (bi, bwd(c), 0))],
        out_specs=pl.BlockSpec((1, q, d_inner), lambda bi, c: (bi, bwd(c), 0)),
        out_shape=jax.ShapeDtypeStruct((bsz, seqlen, d_inner), F32),
        scratch_shapes=[st_shape],
        compiler_params=_params("parallel", "arbitrary"),
        name="ssd_scan_bwd",
    )(xs, bc, bc, dt, a, jnp.asarray(lower.T, BF16), expand(SSD_HEADS), y_f)


def _ssd_out_kernel(h_ref, y_ref, z_ref, g_ref, w_ref, o_ref):
    z = z_ref[...]
    y = _rms(y_ref[...] * (z * _sigmoid(z)), g_ref[...]).astype(BF16)
    o_ref[...] = h_ref[...] + _dot(y, w_ref[...])


def _ssd_out(h, y, z, norm_g, w_out, tm=512):
    t, d = h.shape
    d_inner = y.shape[1]
    row = lambda i: (i, 0)
    return pl.pallas_call(
        _ssd_out_kernel,
        grid=(t // tm,),
        in_specs=[pl.BlockSpec((tm, d), row), pl.BlockSpec((tm, d_inner), row), pl.BlockSpec((tm, d_inner), row),
                  _const_spec((1, d_inner)), _const_spec((d_inner, d))],
        out_specs=pl.BlockSpec((tm, d), row),
        out_shape=jax.ShapeDtypeStruct((t, d), F32),
        compiler_params=_params("parallel"),
        name="ssd_out",
    )(h, y, z, norm_g.reshape(1, d_inner), w_out.astype(BF16))


def _ssd_mixer(h, bsz, mix_g, w_in, conv_w, conv_b, dt_bias, a_log, d_skip, norm_g, w_out):
    t, d = h.shape
    seqlen = t // bsz
    d_inner = SSD_HEADS * SSD_HEAD_DIM
    bcw = SSD_GROUPS * SSD_D_STATE
    conv_dim = d_inner + 4 * bcw
    z, xbc, dt = _ssd_in(h, mix_g, w_in, dt_bias, d_inner, conv_dim)
    xbc = xbc.reshape(bsz, seqlen, conv_dim)
    xs = _conv_silu(xbc, conv_w, conv_b, 0, d_inner, F32)
    bc = _conv_silu(xbc, conv_w, conv_b, d_inner, 4 * bcw, BF16)
    y = _ssd_scan(xs, bc, dt.reshape(bsz, seqlen, 128), a_log, d_skip)
    return _ssd_out(h, y.reshape(t, d_inner), z, norm_g, w_out)


def _na_qkv_kernel(h_ref, g_ref, w_ref, seg_ref, segt_ref, qg_ref, kg_ref, q_ref, k_ref, v_ref):
    xn = _rms(h_ref[...], g_ref[...]).astype(BF16)
    d = q_ref.shape[1]

    def head_norm(x, gain):
        hi, lo = _split2(x * x)
        ssum = _dot(hi, seg_ref[...]) + _dot(lo, seg_ref[...])
        inv = lax.rsqrt(ssum * (1.0 / NA_HEAD_DIM) + RMS_EPS)
        i0, i1, i2 = _split3(inv)
        inv_x = _dot(i0, segt_ref[...]) + _dot(i1, segt_ref[...]) + _dot(i2, segt_ref[...])
        return x * inv_x * gain

    q = _dot(xn, w_ref[:, 0:d])
    q_ref[...] = (head_norm(q, qg_ref[...]) * (NA_HEAD_DIM ** -0.5)).astype(BF16)
    k = _dot(xn, w_ref[:, d:2 * d])
    k_ref[...] = head_norm(k, kg_ref[...]).astype(BF16)
    v_ref[...] = _dot(xn, w_ref[:, 2 * d:3 * d]).astype(BF16)


def _na_qkv(h, norm_g, w_qkv, q_norm, k_norm, tm=512):
    t, d = h.shape
    n_heads = d // NA_HEAD_DIM
    seg = (np.arange(d)[:, None] // NA_HEAD_DIM == np.arange(128)[None, :]).astype(np.float32)
    row = lambda i: (i, 0)
    out = jax.ShapeDtypeStruct((t, d), BF16)
    return pl.pallas_call(
        _na_qkv_kernel,
        grid=(t // tm,),
        in_specs=[pl.BlockSpec((tm, d), row), _const_spec((1, d)), _const_spec((d, 3 * d)), _const_spec((d, 128)),
                  _const_spec((128, d)), _const_spec((1, d)), _const_spec((1, d))],
        out_specs=[pl.BlockSpec((tm, d), row)] * 3,
        out_shape=[out, out, out],
        compiler_params=_params("parallel"),
        name="na_qkv",
    )(h, norm_g.reshape(1, d), w_qkv.astype(BF16), jnp.asarray(seg, BF16), jnp.asarray(seg.T, BF16),
      jnp.tile(q_norm, n_heads).reshape(1, d), jnp.tile(k_norm, n_heads).reshape(1, d))


def _na_attn_kernel(q_ref, k_ref, v_ref, bias_ref, o_ref, *, rows):
    r = pl.program_id(1)
    r0 = jnp.clip(r - NA_WIN_ROWS // 2, 0, rows - NA_WIN_ROWS)
    start = pl.multiple_of(r0 * GRID_W, GRID_W)
    nk = NA_WIN_ROWS * GRID_W
    n_pairs = q_ref.shape[2] // 128
    low = lax.broadcasted_iota(jnp.int32, (GRID_W, 128), 1) < NA_HEAD_DIM
    for hp in range(n_pairs):
        q2 = q_ref[0, :, hp * 128:(hp + 1) * 128]
        zero = jnp.zeros_like(q2)
        qab = jnp.concatenate([jnp.where(low, q2, zero), jnp.where(low, zero, q2)], axis=0)
        k2 = k_ref[0, pl.ds(start, nk), hp * 128:(hp + 1) * 128]
        v2 = v_ref[0, pl.ds(start, nk), hp * 128:(hp + 1) * 128]
        s = lax.dot_general(qab, k2, (((1,), (1,)), ((), ())), preferred_element_type=F32)
        s = s + bias_ref[0, hp]
        e = jnp.exp(s - jnp.max(s, axis=-1, keepdims=True))
        inv = 1.0 / jnp.sum(e, axis=-1, keepdims=True)
        o2 = _dot(e.astype(BF16), v2) * inv
        o_ref[0, :, hp * 128:(hp + 1) * 128] = jnp.where(low, o2[:GRID_W], o2[GRID_W:]).astype(o_ref.dtype)


def _na_bias_table(rpb):
    n_heads = rpb.shape[0]
    qc = np.arange(GRID_W)[:, None]
    kc = np.arange(GRID_W)[None, :]
    win_c0 = np.clip(qc - NA_WIN_COLS // 2, 0, GRID_W - NA_WIN_COLS)
    valid = (kc >= win_c0) & (kc < win_c0 + NA_WIN_COLS)
    col_rel = np.clip(kc - qc + NA_WIN_COLS - 1, 0, 2 * NA_WIN_COLS - 2)
    row_rel = np.arange(NA_WIN_ROWS)[:, None] + np.arange(NA_WIN_ROWS)[None, :]
    tab = rpb.astype(F32)[:, row_rel][:, :, :, col_rel]
    tab = jnp.where(jnp.asarray(valid), tab, MASK_VALUE)
    tab = tab.transpose(1, 0, 3, 2, 4).reshape(NA_WIN_ROWS, n_heads // 2, 2 * GRID_W, NA_WIN_ROWS * GRID_W)
    return tab


def _na_attn(q, k, v, rpb):
    bsz, seqlen, d = q.shape
    rows = seqlen // GRID_W
    assert rows >= NA_WIN_ROWS
    bias = _na_bias_table(rpb)

    def bias_idx(bi, r):
        r0 = jnp.clip(r - NA_WIN_ROWS // 2, 0, rows - NA_WIN_ROWS)
        return (r0 - r + NA_WIN_ROWS - 1, 0, 0, 0)

    return pl.pallas_call(
        functools.partial(_na_attn_kernel, rows=rows),
        grid=(bsz, rows),
        in_specs=[pl.BlockSpec((1, GRID_W, d), lambda bi, r: (bi, r, 0)),
                  pl.BlockSpec((1, seqlen, d), lambda bi, r: (bi, 0, 0)),
                  pl.BlockSpec((1, seqlen, d), lambda bi, r: (bi, 0, 0)),
                  pl.BlockSpec((1,) + bias.shape[1:], bias_idx)],
        out_specs=pl.BlockSpec((1, GRID_W, d), lambda bi, r: (bi, r, 0)),
        out_shape=jax.ShapeDtypeStruct((bsz, seqlen, d), BF16),
        compiler_params=_params("parallel", "arbitrary"),
        name="na_attn",
    )(q, k, v, bias)


def _proj_residual_kernel(h_ref, x_ref, w_ref, o_ref):
    o_ref[...] = h_ref[...] + _dot(x_ref[...], w_ref[...])


def _proj_residual(h, x, w, tm=512):
    t, d = h.shape
    row = lambda i: (i, 0)
    return pl.pallas_call(
        _proj_residual_kernel,
        grid=(t // tm,),
        in_specs=[pl.BlockSpec((tm, d), row), pl.BlockSpec((tm, x.shape[1]), row), _const_spec(w.shape)],
        out_specs=pl.BlockSpec((tm, d), row),
        out_shape=jax.ShapeDtypeStruct((t, d), F32),
        compiler_params=_params("parallel"),
        name="proj_residual",
    )(h, x, w.astype(BF16))


def _na_mixer(h, bsz, mix_g, w_qkv, q_norm, k_norm, rpb, w_out):
    t, d = h.shape
    seqlen = t // bsz
    q, k, v = _na_qkv(h, mix_g, w_qkv, q_norm, k_norm)
    shp = (bsz, seqlen, d)
    o = _na_attn(q.reshape(shp), k.reshape(shp), v.reshape(shp), rpb)
    return _proj_residual(h, o.reshape(t, d), w_out)


def kernel(x, p, ffn1_norm, ffn1_w_gu, ffn1_w_down, mix_norm, ffn2_norm, ffn2_w_gu, ffn2_w_down, ple_norm, ple_w_gate, ple_w_proj, ple_post_norm, ssd_w_in, ssd_conv_w, ssd_conv_b, ssd_dt_bias, ssd_a_log, ssd_d, ssd_norm, ssd_w_out, na_w_qkv, na_q_norm, na_k_norm, na_rpb, na_w_out):
    bsz, seqlen, d = x.shape
    depth = p.shape[0]
    t = bsz * seqlen
    h = x.reshape(t, d)
    for i in range(depth):
        h = _ffn(h, ffn1_norm[i], ffn1_w_gu[i], ffn1_w_down[i])
        j = i // 2
        if i % 2 == 0:
            h = _ssd_mixer(h, bsz, mix_norm[i], ssd_w_in[j], ssd_conv_w[j], ssd_conv_b[j], ssd_dt_bias[j],
                           ssd_a_log[j], ssd_d[j], ssd_norm[j], ssd_w_out[j])
        else:
            h = _na_mixer(h, bsz, mix_norm[i], na_w_qkv[j], na_q_norm[j], na_k_norm[j], na_rpb[j], na_w_out[j])
        h = _ffn(h, ffn2_norm[i], ffn2_w_gu[i], ffn2_w_down[i],
                 ple=(p[i].reshape(t, -1), ple_norm[i], ple_w_gate[i], ple_w_proj[i], ple_post_norm[i]))
    return h.reshape(bsz, seqlen, d)
```

```python
import functools
import math

import jax
import jax.numpy as jnp
import numpy as np
from jax import lax
from jax.experimental import pallas as pl
from jax.experimental.pallas import tpu as pltpu

F32 = jnp.float32
BF16 = jnp.bfloat16

RMS_EPS = 1e-6
SSD_HEAD_DIM = 64
SSD_HEADS = 32
SSD_GROUPS = 8
SSD_D_STATE = 128
SSD_CONV = 5
SSD_CHUNK = 128
GRID_W = 64
NA_HEAD_DIM = 64
NA_WIN_ROWS = 8
NA_WIN_COLS = 16
FFN_CHUNK = 256
CONV_HALO = 8
CONV_ROW_BLOCK = 32
VMEM_LIMIT = 56 * 1024 * 1024
MASK_VALUE = -1e30
LOG2E = math.log2(math.e)


def _params(*sem):
    return pltpu.CompilerParams(dimension_semantics=sem, vmem_limit_bytes=VMEM_LIMIT)


def _const_spec(shape):
    nd = len(shape)
    return pl.BlockSpec(shape, lambda *_: (0,) * nd, pipeline_mode=pl.Buffered(1))


def _rms(x, g):
    ms = jnp.mean(x * x, axis=-1, keepdims=True)
    return x * lax.rsqrt(ms + RMS_EPS) * g


def _sigmoid(x):
    return 1.0 / (1.0 + jnp.exp(-x))


def _dot(a, b):
    return jnp.dot(a, b, preferred_element_type=F32)


def _split2(x):
    hi = x.astype(BF16)
    lo = (x - hi.astype(F32)).astype(BF16)
    return hi, lo


def _split3(x):
    hi = x.astype(BF16)
    r = x - hi.astype(F32)
    mid = r.astype(BF16)
    lo = (r - mid.astype(F32)).astype(BF16)
    return hi, mid, lo


def _ffn_body(h_ref, g_ref, wgu_ref, wd_ref, a_ref):
    x = h_ref[...]
    xn = _rms(x, g_ref[...]).astype(BF16)
    n_chunks = wgu_ref.shape[0]
    for c in range(n_chunks):
        gu = _dot(xn, wgu_ref[c])
        gate = gu[:, :FFN_CHUNK]
        up = gu[:, FFN_CHUNK:]
        a_ref[:, c * FFN_CHUNK:(c + 1) * FFN_CHUNK] = (gate * _sigmoid(gate) * up).astype(BF16)
    return x + 0.5 * _dot(a_ref[...], wd_ref[...])


def _ffn_kernel(h_ref, g_ref, wgu_ref, wd_ref, o_ref, a_ref):
    o_ref[...] = _ffn_body(h_ref, g_ref, wgu_ref, wd_ref, a_ref)


def _ffn_ple_kernel(h_ref, g_ref, wgu_ref, wd_ref, p_ref, pg_ref, wgate_ref, wproj_ref, ppg_ref, o_ref, a_ref):
    h2 = _ffn_body(h_ref, g_ref, wgu_ref, wd_ref, a_ref)
    gate = _sigmoid(_dot(_rms(h2, pg_ref[...]).astype(BF16), wgate_ref[...]))
    emb = _dot(p_ref[...].astype(BF16), wproj_ref[...])
    o_ref[...] = h2 + gate * _rms(emb, ppg_ref[...])


def _prep_ffn_weights(w_gu, w_down):
    d_model, two_ff = w_gu.shape
    d_ff = two_ff // 2
    n_chunks = d_ff // FFN_CHUNK
    wg = w_gu[:, :d_ff].reshape(d_model, n_chunks, FFN_CHUNK)
    wu = w_gu[:, d_ff:].reshape(d_model, n_chunks, FFN_CHUNK)
    wgu = jnp.concatenate([wg, wu], axis=-1).transpose(1, 0, 2).astype(BF16)
    return wgu, w_down.astype(BF16)


def _ffn(h, norm_g, w_gu, w_down, ple=None, tm=512):
    t, d = h.shape
    wgu, wd = _prep_ffn_weights(w_gu, w_down)
    d_ff = wd.shape[0]
    row = lambda i: (i, 0)
    in_specs = [pl.BlockSpec((tm, d), row), _const_spec((1, d)), _const_spec(wgu.shape), _const_spec(wd.shape)]
    args = [h, norm_g.reshape(1, d), wgu, wd]
    kern = _ffn_kernel
    if ple is not None:
        p, ple_g, w_gate, w_proj, post_g = ple
        dp = p.shape[-1]
        in_specs += [pl.BlockSpec((tm, dp), row), _const_spec((1, d)), _const_spec((d, d)),
                     _const_spec((dp, d)), _const_spec((1, d))]
        args += [p, ple_g.reshape(1, d), w_gate.astype(BF16), w_proj.astype(BF16), post_g.reshape(1, d)]
        kern = _ffn_ple_kernel
    return pl.pallas_call(
        kern,
        grid=(t // tm,),
        in_specs=in_specs,
        out_specs=pl.BlockSpec((tm, d), row),
        out_shape=jax.ShapeDtypeStruct((t, d), F32),
        scratch_shapes=[pltpu.VMEM((tm, d_ff), BF16)],
        compiler_params=_params("parallel"),
        name="ffn_ple" if ple is not None else "ffn",
    )(*args)


def _ssd_in_kernel(h_ref, prev_ref, next_ref, g_ref, wz_ref, wx_ref, wdt_ref, dtb_ref, cw_ref, cb_ref,
                   z_ref, xs_ref, bc_ref, dt_ref, ext_ref, *, tiles_per_seq):
    tm = h_ref.shape[0]
    ext_rows = ext_ref.shape[0]
    half = SSD_CONV // 2
    pos = lax.rem(pl.program_id(0), tiles_per_seq)
    g = g_ref[...]
    x = h_ref[...]
    xn = _rms(x, g).astype(BF16)
    z_ref[...] = _dot(xn, wz_ref[...])
    raw = _dot(xn, wdt_ref[...]) + dtb_ref[...]
    dt_ref[...] = jnp.maximum(raw, 0.0) + jnp.log1p(jnp.exp(-jnp.abs(raw)))

    ext_ref[0:half, :] = jnp.where(pos > 0, prev_ref[CONV_HALO - half:, :], 0.0)
    ext_ref[half:half + tm, :] = x
    ext_ref[half + tm:half + tm + CONV_HALO, :] = jnp.where(pos < tiles_per_seq - 1, next_ref[...], 0.0)
    ext_ref[half + tm + CONV_HALO:, :] = jnp.zeros((ext_rows - half - tm - CONV_HALO, x.shape[1]), F32)
    xe = _rms(ext_ref[...], g).astype(BF16)

    n_chunks = wx_ref.shape[0]
    cw = wx_ref.shape[2]
    n_xs = xs_ref.shape[1] // cw
    rb = CONV_ROW_BLOCK
    for c in range(n_chunks):
        proj = _dot(xe, wx_ref[c])
        w = cw_ref[:, c * cw:(c + 1) * cw]
        b = cb_ref[:, c * cw:(c + 1) * cw]
        for r in range(tm // rb):
            blk = proj[r * rb:r * rb + rb + CONV_HALO]
            acc = blk * w[SSD_CONV - 1:SSD_CONV]
            for j in range(SSD_CONV - 2, -1, -1):
                acc = blk * w[j:j + 1] + pltpu.roll(acc, rb + CONV_HALO - 1, axis=0)
            acc = acc[0:rb] + b
            out = acc * _sigmoid(acc)
            if c < n_xs:
                xs_ref[r * rb:(r + 1) * rb, c * cw:(c + 1) * cw] = out
            else:
                bc_ref[r * rb:(r + 1) * rb, (c - n_xs) * cw:(c - n_xs + 1) * cw] = out.astype(BF16)


def _ssd_in(h, seqlen, norm_g, w_in, dt_bias, conv_w, conv_b, d_inner, conv_dim, tm=256, cw=512):
    t, d = h.shape
    n_dt = w_in.shape[1] - d_inner - conv_dim
    wz = w_in[:, :d_inner].astype(BF16)
    wx = w_in[:, d_inner:d_inner + conv_dim].reshape(d, conv_dim // cw, cw).transpose(1, 0, 2).astype(BF16)
    wdt = jnp.pad(w_in[:, d_inner + conv_dim:], ((0, 0), (0, 128 - n_dt))).astype(BF16)
    dtb = jnp.pad(dt_bias.reshape(1, n_dt), ((0, 0), (0, 128 - n_dt)))
    cwt = jnp.pad(conv_w, ((0, 8 - SSD_CONV), (0, 0)))
    hb = tm // CONV_HALO
    n_hblk = t // CONV_HALO
    row = lambda i: (i, 0)
    return pl.pallas_call(
        functools.partial(_ssd_in_kernel, tiles_per_seq=seqlen // tm),
        grid=(t // tm,),
        in_specs=[pl.BlockSpec((tm, d), row),
                  pl.BlockSpec((CONV_HALO, d), lambda i: (jnp.maximum(i * hb - 1, 0), 0)),
                  pl.BlockSpec((CONV_HALO, d), lambda i: (jnp.minimum((i + 1) * hb, n_hblk - 1), 0)),
                  _const_spec((1, d)), _const_spec(wz.shape), _const_spec(wx.shape),
                  _const_spec(wdt.shape), _const_spec((1, 128)), _const_spec(cwt.shape), _const_spec((1, conv_dim))],
        out_specs=[pl.BlockSpec((tm, d_inner), row), pl.BlockSpec((tm, d_inner), row),
                   pl.BlockSpec((tm, conv_dim - d_inner), row), pl.BlockSpec((tm, 128), row)],
        out_shape=[jax.ShapeDtypeStruct((t, d_inner), F32), jax.ShapeDtypeStruct((t, d_inner), F32),
                   jax.ShapeDtypeStruct((t, conv_dim - d_inner), BF16), jax.ShapeDtypeStruct((t, 128), F32)],
        scratch_shapes=[pltpu.VMEM((tm + 2 * CONV_HALO, d), F32)],
        compiler_params=_params("parallel"),
        name="ssd_in",
    )(h, h, h, norm_g.reshape(1, d), wz, wx, wdt, dtb, cwt, conv_b.reshape(1, conv_dim))


def _ssd_scan_kernel(xs_ref, b_ref, c_ref, dt_ref, a_ref, tri_ref, e_ref, extra_ref, dsk_ref, y_ref, st_ref,
                     *, reverse, col0):
    q = SSD_CHUNK
    hg = SSD_HEADS // SSD_GROUPS
    gw = hg * SSD_HEAD_DIM

    @pl.when(pl.program_id(1) == 0)
    def _():
        st_ref[...] = jnp.zeros_like(st_ref)

    dt = dt_ref[0]
    dta = dt * a_ref[...]
    hi, mid, lo = _split3(dta)
    tri = tri_ref[...]
    cum = _dot(tri, hi) + _dot(tri, mid) + _dot(tri, lo)
    cum_t = cum.T
    ecum = jnp.exp(cum)
    edge = cum[q - 1:q, :] if not reverse else cum[0:1, :]
    dt_dec = dt * jnp.exp(edge - cum)

    parts = _split2(dt) + _split2(dt_dec) + _split2(ecum)
    stacked = jnp.concatenate(parts, axis=0)
    ex = _dot(stacked, e_ref[...])
    dt_x = ex[0:q] + ex[q:2 * q]
    dd_x = ex[2 * q:3 * q] + ex[3 * q:4 * q]
    ec_x = ex[4 * q:5 * q] + ex[5 * q:6 * q]
    edge_x = ec_x[q - 1:q, :] if not reverse else ec_x[0:1, :]

    xs = xs_ref[0]
    xdt = (xs * dt_x).astype(BF16)
    xdd = (xs * dd_x).astype(BF16)

    rows = lax.broadcasted_iota(jnp.int32, (q, q), 0)
    cols = lax.broadcasted_iota(jnp.int32, (q, q), 1)
    causal = (rows >= cols) if not reverse else (rows <= cols)
    lane_head = lax.broadcasted_iota(jnp.int32, (q, gw), 1) // SSD_HEAD_DIM

    for g in range(SSD_GROUPS):
        bg = b_ref[0, :, g * SSD_D_STATE:(g + 1) * SSD_D_STATE]
        cg = c_ref[0, :, g * SSD_D_STATE:(g + 1) * SSD_D_STATE]
        cb = lax.dot_general(cg, bg, (((1,), (1,)), ((), ())), preferred_element_type=F32)
        xdt_g = xdt[:, g * gw:(g + 1) * gw]
        scores, blocks = [], []
        for j in range(hg):
            col = col0 + g * hg + j
            seg = cum[:, col:col + 1] - cum_t[col:col + 1, :]
            decay = jnp.exp(jnp.where(causal, seg, -jnp.inf))
            scores.append((cb * decay).astype(BF16))
            blocks.append(jnp.where(lane_head == j, xdt_g, jnp.zeros_like(xdt_g)))
        y_diag = _dot(jnp.concatenate(scores, axis=1), jnp.concatenate(blocks, axis=0))
        st = st_ref[g]
        y_off = _dot(cg, st.astype(BF16)) * ec_x[:, g * gw:(g + 1) * gw]
        y = y_diag + y_off
        if extra_ref is not None:
            y = y + extra_ref[0, :, g * gw:(g + 1) * gw]
        if dsk_ref is not None:
            y = y + xs[:, g * gw:(g + 1) * gw] * dsk_ref[:, g * gw:(g + 1) * gw]
        y_ref[0, :, g * gw:(g + 1) * gw] = y
        new = lax.dot_general(bg, xdd[:, g * gw:(g + 1) * gw], (((0,), (0,)), ((), ())),
                              preferred_element_type=F32)
        st_ref[g] = st * edge_x[:, g * gw:(g + 1) * gw] + new


def _scan_fwd_kernel(xs_ref, b_ref, c_ref, dt_ref, a_ref, tri_ref, e_ref, dsk_ref, y_ref, st_ref):
    _ssd_scan_kernel(xs_ref, b_ref, c_ref, dt_ref, a_ref, tri_ref, e_ref, None, dsk_ref, y_ref, st_ref,
                     reverse=False, col0=0)


def _scan_bwd_kernel(xs_ref, b_ref, c_ref, dt_ref, a_ref, tri_ref, e_ref, yf_ref, y_ref, st_ref):
    _ssd_scan_kernel(xs_ref, b_ref, c_ref, dt_ref, a_ref, tri_ref, e_ref, yf_ref, None, y_ref, st_ref,
                     reverse=True, col0=SSD_HEADS)


def _ssd_scan(xs, bc, dt, a_log, d_skip):
    bsz, seqlen, d_inner = xs.shape
    q = SSD_CHUNK
    nc = seqlen // q
    bcw = SSD_GROUPS * SSD_D_STATE
    a = jnp.pad(-jnp.exp(a_log.astype(F32)).reshape(1, 2 * SSD_HEADS), ((0, 0), (0, 128 - 2 * SSD_HEADS)))
    dsk = jnp.repeat(d_skip.astype(F32), SSD_HEAD_DIM).reshape(1, d_inner)
    lower = np.tril(np.ones((q, q), np.float32))
    chan_head = np.arange(d_inner) // SSD_HEAD_DIM
    st_shape = pltpu.VMEM((SSD_GROUPS, SSD_D_STATE, d_inner // SSD_GROUPS), F32)

    def expand(col0):
        e = (np.arange(128)[:, None] == (chan_head[None, :] + col0)).astype(np.float32)
        return jnp.asarray(e, BF16)

    def specs(cidx, b_blk, c_blk):
        return [
            pl.BlockSpec((1, q, d_inner), lambda bi, c: (bi, cidx(c), 0)),
            pl.BlockSpec((1, q, bcw), lambda bi, c: (bi, cidx(c), b_blk)),
            pl.BlockSpec((1, q, bcw), lambda bi, c: (bi, cidx(c), c_blk)),
            pl.BlockSpec((1, q, 128), lambda bi, c: (bi, cidx(c), 0)),
            _const_spec((1, 128)), _const_spec((q, q)), _const_spec((128, d_inner)),
        ]

    fwd = lambda c: c
    y_f = pl.pallas_call(
        _scan_fwd_kernel,
        grid=(bsz, nc),
        in_specs=specs(fwd, 0, 1) + [_const_spec((1, d_inner))],
        out_specs=pl.BlockSpec((1, q, d_inner), lambda bi, c: (bi, c, 0)),
        out_shape=jax.ShapeDtypeStruct((bsz, seqlen, d_inner), F32),
        scratch_shapes=[st_shape],
        compiler_params=_params("parallel", "arbitrary"),
        name="ssd_scan_fwd",
    )(xs, bc, bc, dt, a, jnp.asarray(lower, BF16), expand(0), dsk)
    bwd = lambda c: nc - 1 - c
    return pl.pallas_call(
        _scan_bwd_kernel,
        grid=(bsz, nc),
        in_specs=specs(bwd, 2, 3) + [pl.BlockSpec((1, q, d_inner), lambda bi, c: (bi, bwd(c), 0))],
        out_specs=pl.BlockSpec((1, q, d_inner), lambda bi, c: (bi, bwd(c), 0)),
        out_shape=jax.ShapeDtypeStruct((bsz, seqlen, d_inner), F32),
        scratch_shapes=[st_shape],
        compiler_params=_params("parallel", "arbitrary"),
        name="ssd_scan_bwd",
    )(xs, bc, bc, dt, a, jnp.asarray(lower.T, BF16), expand(SSD_HEADS), y_f)


def _ssd_out_kernel(h_ref, y_ref, z_ref, g_ref, w_ref, o_ref):
    z = z_ref[...]
    y = _rms(y_ref[...] * (z * _sigmoid(z)), g_ref[...]).astype(BF16)
    o_ref[...] = h_ref[...] + _dot(y, w_ref[...])


def _ssd_out(h, y, z, norm_g, w_out, tm=512):
    t, d = h.shape
    d_inner = y.shape[1]
    row = lambda i: (i, 0)
    return pl.pallas_call(
        _ssd_out_kernel,
        grid=(t // tm,),
        in_specs=[pl.BlockSpec((tm, d), row), pl.BlockSpec((tm, d_inner), row), pl.BlockSpec((tm, d_inner), row),
                  _const_spec((1, d_inner)), _const_spec((d_inner, d))],
        out_specs=pl.BlockSpec((tm, d), row),
        out_shape=jax.ShapeDtypeStruct((t, d), F32),
        compiler_params=_params("parallel"),
        name="ssd_out",
    )(h, y, z, norm_g.reshape(1, d_inner), w_out.astype(BF16))


def _ssd_mixer(h, bsz, mix_g, w_in, conv_w, conv_b, dt_bias, a_log, d_skip, norm_g, w_out):
    t, d = h.shape
    seqlen = t // bsz
    d_inner = SSD_HEADS * SSD_HEAD_DIM
    bcw = SSD_GROUPS * SSD_D_STATE
    conv_dim = d_inner + 4 * bcw
    z, xs, bc, dt = _ssd_in(h, seqlen, mix_g, w_in, dt_bias, conv_w, conv_b, d_inner, conv_dim)
    y = _ssd_scan(xs.reshape(bsz, seqlen, d_inner), bc.reshape(bsz, seqlen, 4 * bcw),
                  dt.reshape(bsz, seqlen, 128), a_log, d_skip)
    return _ssd_out(h, y.reshape(t, d_inner), z, norm_g, w_out)


def _na_qkv_kernel(h_ref, g_ref, w_ref, avg_ref, qg_ref, kg_ref, q_ref, k_ref, v_ref):
    xn = _rms(h_ref[...], g_ref[...]).astype(BF16)
    d = q_ref.shape[1]
    gw = avg_ref.shape[0]

    def head_norm(x, gain):
        hi, lo = _split2(x * x)
        ms = _dot(hi, avg_ref[...]) + _dot(lo, avg_ref[...])
        return x * lax.rsqrt(ms + RMS_EPS) * gain

    for c in range(d // gw):
        q = _dot(xn, w_ref[:, c * gw:(c + 1) * gw])
        q = head_norm(q, qg_ref[:, c * gw:(c + 1) * gw]) * (NA_HEAD_DIM ** -0.5 * LOG2E)
        q_ref[:, c * gw:(c + 1) * gw] = q.astype(BF16)
        k = _dot(xn, w_ref[:, d + c * gw:d + (c + 1) * gw])
        k_ref[:, c * gw:(c + 1) * gw] = head_norm(k, kg_ref[:, c * gw:(c + 1) * gw]).astype(BF16)
    v_ref[...] = _dot(xn, w_ref[:, 2 * d:3 * d]).astype(BF16)


def _na_qkv(h, norm_g, w_qkv, q_norm, k_norm, tm=512, gw=256):
    t, d = h.shape
    n_heads = d // NA_HEAD_DIM
    same_head = np.arange(gw)[:, None] // NA_HEAD_DIM == np.arange(gw)[None, :] // NA_HEAD_DIM
    avg = jnp.asarray(same_head.astype(np.float32) / NA_HEAD_DIM, BF16)
    row = lambda i: (i, 0)
    out = jax.ShapeDtypeStruct((t, d), BF16)
    return pl.pallas_call(
        _na_qkv_kernel,
        grid=(t // tm,),
        in_specs=[pl.BlockSpec((tm, d), row), _const_spec((1, d)), _const_spec((d, 3 * d)), _const_spec((gw, gw)),
                  _const_spec((1, d)), _const_spec((1, d))],
        out_specs=[pl.BlockSpec((tm, d), row)] * 3,
        out_shape=[out, out, out],
        compiler_params=_params("parallel"),
        name="na_qkv",
    )(h, norm_g.reshape(1, d), w_qkv.astype(BF16), avg,
      jnp.tile(q_norm, n_heads).reshape(1, d), jnp.tile(k_norm, n_heads).reshape(1, d))


def _na_attn_kernel(q_ref, k_ref, v_ref, bias_ref, o_ref, s_ref, p_ref, *, rows):
    r = pl.program_id(1)
    r0 = jnp.clip(r - NA_WIN_ROWS // 2, 0, rows - NA_WIN_ROWS)
    start = pl.multiple_of(r0 * GRID_W, GRID_W)
    nk = NA_WIN_ROWS * GRID_W
    n_pairs = q_ref.shape[2] // 128
    low = lax.broadcasted_iota(jnp.int32, (GRID_W, 128), 1) < NA_HEAD_DIM

    row_max = []
    for hp in range(n_pairs):
        q2 = q_ref[0, :, hp * 128:(hp + 1) * 128]
        zero = jnp.zeros_like(q2)
        qab = jnp.concatenate([jnp.where(low, q2, zero), jnp.where(low, zero, q2)], axis=0)
        k2 = k_ref[0, pl.ds(start, nk), hp * 128:(hp + 1) * 128]
        s = lax.dot_general(qab, k2, (((1,), (1,)), ((), ())), preferred_element_type=F32)
        s = s + bias_ref[0, hp]
        s_ref[hp] = s
        row_max.append(jnp.max(s, axis=-1, keepdims=True))
    inv_sum = []
    for hp in range(n_pairs):
        e = jnp.exp2(s_ref[hp] - row_max[hp])
        inv_sum.append(1.0 / jnp.sum(e, axis=-1, keepdims=True))
        p_ref[hp] = e.astype(BF16)
    for hp in range(n_pairs):
        v2 = v_ref[0, pl.ds(start, nk), hp * 128:(hp + 1) * 128]
        o2 = _dot(p_ref[hp], v2) * inv_sum[hp]
        o_ref[0, :, hp * 128:(hp + 1) * 128] = jnp.where(low, o2[:GRID_W], o2[GRID_W:]).astype(o_ref.dtype)


def _na_bias_table(rpb):
    n_heads = rpb.shape[0]
    qc = np.arange(GRID_W)[:, None]
    kc = np.arange(GRID_W)[None, :]
    win_c0 = np.clip(qc - NA_WIN_COLS // 2, 0, GRID_W - NA_WIN_COLS)
    valid = (kc >= win_c0) & (kc < win_c0 + NA_WIN_COLS)
    col_rel = np.clip(kc - qc + NA_WIN_COLS - 1, 0, 2 * NA_WIN_COLS - 2)
    row_rel = np.arange(NA_WIN_ROWS)[:, None] + np.arange(NA_WIN_ROWS)[None, :]
    tab = (rpb.astype(F32) * LOG2E)[:, row_rel][:, :, :, col_rel]
    tab = jnp.where(jnp.asarray(valid), tab, MASK_VALUE)
    tab = tab.transpose(1, 0, 3, 2, 4).reshape(NA_WIN_ROWS, n_heads // 2, 2 * GRID_W, NA_WIN_ROWS * GRID_W)
    return tab


def _na_attn(q, k, v, rpb):
    bsz, seqlen, d = q.shape
    rows = seqlen // GRID_W
    assert rows >= NA_WIN_ROWS
    bias = _na_bias_table(rpb)

    def bias_idx(bi, r):
        r0 = jnp.clip(r - NA_WIN_ROWS // 2, 0, rows - NA_WIN_ROWS)
        return (r0 - r + NA_WIN_ROWS - 1, 0, 0, 0)

    return pl.pallas_call(
        functools.partial(_na_attn_kernel, rows=rows),
        grid=(bsz, rows),
        in_specs=[pl.BlockSpec((1, GRID_W, d), lambda bi, r: (bi, r, 0)),
                  pl.BlockSpec((1, seqlen, d), lambda bi, r: (bi, 0, 0)),
                  pl.BlockSpec((1, seqlen, d), lambda bi, r: (bi, 0, 0)),
                  pl.BlockSpec((1,) + bias.shape[1:], bias_idx)],
        out_specs=pl.BlockSpec((1, GRID_W, d), lambda bi, r: (bi, r, 0)),
        out_shape=jax.ShapeDtypeStruct((bsz, seqlen, d), BF16),
        scratch_shapes=[pltpu.VMEM((d // 128, 2 * GRID_W, NA_WIN_ROWS * GRID_W), F32),
                        pltpu.VMEM((d // 128, 2 * GRID_W, NA_WIN_ROWS * GRID_W), BF16)],
        compiler_params=_params("parallel", "arbitrary"),
        name="na_attn",
    )(q, k, v, bias)


def _proj_residual_kernel(h_ref, x_ref, w_ref, o_ref):
    o_ref[...] = h_ref[...] + _dot(x_ref[...], w_ref[...])


def _proj_residual(h, x, w, tm=512):
    t, d = h.shape
    row = lambda i: (i, 0)
    return pl.pallas_call(
        _proj_residual_kernel,
        grid=(t // tm,),
        in_specs=[pl.BlockSpec((tm, d), row), pl.BlockSpec((tm, x.shape[1]), row), _const_spec(w.shape)],
        out_specs=pl.BlockSpec((tm, d), row),
        out_shape=jax.ShapeDtypeStruct((t, d), F32),
        compiler_params=_params("parallel"),
        name="proj_residual",
    )(h, x, w.astype(BF16))


def _na_mixer(h, bsz, mix_g, w_qkv, q_norm, k_norm, rpb, w_out):
    t, d = h.shape
    seqlen = t // bsz
    q, k, v = _na_qkv(h, mix_g, w_qkv, q_norm, k_norm)
    shp = (bsz, seqlen, d)
    o = _na_attn(q.reshape(shp), k.reshape(shp), v.reshape(shp), rpb)
    return _proj_residual(h, o.reshape(t, d), w_out)


def kernel(x, p, ffn1_norm, ffn1_w_gu, ffn1_w_down, mix_norm, ffn2_norm, ffn2_w_gu, ffn2_w_down, ple_norm, ple_w_gate, ple_w_proj, ple_post_norm, ssd_w_in, ssd_conv_w, ssd_conv_b, ssd_dt_bias, ssd_a_log, ssd_d, ssd_norm, ssd_w_out, na_w_qkv, na_q_norm, na_k_norm, na_rpb, na_w_out):
    bsz, seqlen, d = x.shape
    depth = p.shape[0]
    t = bsz * seqlen
    h = x.reshape(t, d)
    for i in range(depth):
        h = _ffn(h, ffn1_norm[i], ffn1_w_gu[i], ffn1_w_down[i])
        j = i // 2
        if i % 2 == 0:
            h = _ssd_mixer(h, bsz, mix_norm[i], ssd_w_in[j], ssd_conv_w[j], ssd_conv_b[j], ssd_dt_bias[j],
                           ssd_a_log[j], ssd_d[j], ssd_norm[j], ssd_w_out[j])
        else:
            h = _na_mixer(h, bsz, mix_norm[i], na_w_qkv[j], na_q_norm[j], na_k_norm[j], na_rpb[j], na_w_out[j])
        h = _ffn(h, ffn2_norm[i], ffn2_w_gu[i], ffn2_w_down[i],
                 ple=(p[i].reshape(t, -1), ple_norm[i], ple_w_gate[i], ple_w_proj[i], ple_post_norm[i]))
    return h.reshape(bsz, seqlen, d)
```

```python
import functools
import math

import jax
import jax.numpy as jnp
import numpy as np
from jax import lax
from jax.experimental import pallas as pl
from jax.experimental.pallas import tpu as pltpu

F32 = jnp.float32
BF16 = jnp.bfloat16

RMS_EPS = 1e-6
SSD_HEAD_DIM = 64
SSD_HEADS = 32
SSD_GROUPS = 8
SSD_D_STATE = 128
SSD_CONV = 5
SSD_CHUNK = 128
GRID_W = 64
NA_HEAD_DIM = 64
NA_WIN_ROWS = 8
NA_WIN_COLS = 16
NA_ROWS_PER_STEP = 2
FFN_CHUNK = 256
SUBLANES = 8
LANES = 128
CONV_HALO = SUBLANES
SCAN_CHUNKS_PER_STEP = 4
VMEM_LIMIT = 56 * 1024 * 1024
MASK_VALUE = -1e30
LOG2E = math.log2(math.e)


def _params(*sem):
    return pltpu.CompilerParams(dimension_semantics=sem, vmem_limit_bytes=VMEM_LIMIT)


def _const_spec(shape):
    nd = len(shape)
    return pl.BlockSpec(shape, lambda *_: (0,) * nd, pipeline_mode=pl.Buffered(1))


def _rms(x, g):
    ms = jnp.mean(x * x, axis=-1, keepdims=True)
    return x * lax.rsqrt(ms + RMS_EPS) * g


def _sigmoid(x):
    return 1.0 / (1.0 + jnp.exp(-x))


def _dot(a, b):
    return jnp.dot(a, b, preferred_element_type=F32)


def _split2(x):
    hi = x.astype(BF16)
    lo = (x - hi.astype(F32)).astype(BF16)
    return hi, lo


def _split3(x):
    hi = x.astype(BF16)
    r = x - hi.astype(F32)
    mid = r.astype(BF16)
    lo = (r - mid.astype(F32)).astype(BF16)
    return hi, mid, lo


def _chunk_times():
    p = np.arange(SSD_CHUNK)
    return (SSD_CHUNK // SUBLANES) * (p % SUBLANES) + p // SUBLANES


def _ffn_body(h_ref, g_ref, wgu_ref, wd_ref, a_ref):
    x = h_ref[...]
    xn = _rms(x, g_ref[...]).astype(BF16)
    n_chunks = wgu_ref.shape[0]
    for c in range(n_chunks):
        gu = _dot(xn, wgu_ref[c])
        gate = gu[:, :FFN_CHUNK]
        up = gu[:, FFN_CHUNK:]
        a_ref[:, c * FFN_CHUNK:(c + 1) * FFN_CHUNK] = (gate * _sigmoid(gate) * up).astype(BF16)
    return x + 0.5 * _dot(a_ref[...], wd_ref[...])


def _ffn_kernel(h_ref, g_ref, wgu_ref, wd_ref, o_ref, a_ref):
    o_ref[...] = _ffn_body(h_ref, g_ref, wgu_ref, wd_ref, a_ref)


def _ffn_ple_kernel(h_ref, g_ref, wgu_ref, wd_ref, p_ref, pg_ref, wgate_ref, wproj_ref, ppg_ref, o_ref, a_ref):
    h2 = _ffn_body(h_ref, g_ref, wgu_ref, wd_ref, a_ref)
    gate = _sigmoid(_dot(_rms(h2, pg_ref[...]).astype(BF16), wgate_ref[...]))
    emb = _dot(p_ref[...].astype(BF16), wproj_ref[...])
    o_ref[...] = h2 + gate * _rms(emb, ppg_ref[...])


def _prep_ffn_weights(w_gu, w_down):
    d_model, two_ff = w_gu.shape
    d_ff = two_ff // 2
    n_chunks = d_ff // FFN_CHUNK
    wg = w_gu[:, :d_ff].reshape(d_model, n_chunks, FFN_CHUNK)
    wu = w_gu[:, d_ff:].reshape(d_model, n_chunks, FFN_CHUNK)
    wgu = jnp.concatenate([wg, wu], axis=-1).transpose(1, 0, 2).astype(BF16)
    return wgu, w_down.astype(BF16)


def _ffn(h, norm_g, w_gu, w_down, ple=None, tm=512):
    t, d = h.shape
    wgu, wd = _prep_ffn_weights(w_gu, w_down)
    d_ff = wd.shape[0]
    row = lambda i: (i, 0)
    in_specs = [pl.BlockSpec((tm, d), row), _const_spec((1, d)), _const_spec(wgu.shape), _const_spec(wd.shape)]
    args = [h, norm_g.reshape(1, d), wgu, wd]
    kern = _ffn_kernel
    if ple is not None:
        p_all, layer, ple_g, w_gate, w_proj, post_g = ple
        dp = p_all.shape[-1]
        in_specs += [pl.BlockSpec((pl.Squeezed(), tm, dp), lambda i: (layer, i, 0)), _const_spec((1, d)),
                     _const_spec((d, d)), _const_spec((dp, d)), _const_spec((1, d))]
        args += [p_all, ple_g.reshape(1, d), w_gate.astype(BF16), w_proj.astype(BF16), post_g.reshape(1, d)]
        kern = _ffn_ple_kernel
    return pl.pallas_call(
        kern,
        grid=(t // tm,),
        in_specs=in_specs,
        out_specs=pl.BlockSpec((tm, d), row),
        out_shape=jax.ShapeDtypeStruct((t, d), F32),
        scratch_shapes=[pltpu.VMEM((tm, d_ff), BF16)],
        compiler_params=_params("parallel"),
        name="ffn_ple" if ple is not None else "ffn",
    )(*args)


def _ssd_in_kernel(h_ref, prev_ref, next_ref, g_ref, gather_ref, wz_ref, wx_ref, wdt_ref, dtb_ref, cw_ref, cb_ref,
                   z_ref, xs_ref, bc_ref, dt_ref, src_ref, *, tiles_per_seq):
    tm = h_ref.shape[0]
    q = SSD_CHUNK
    n_sub = tm // q
    pos = lax.rem(pl.program_id(0), tiles_per_seq)

    src_ref[0:CONV_HALO, :] = jnp.where(pos > 0, prev_ref[...], 0.0)
    src_ref[CONV_HALO:CONV_HALO + tm, :] = h_ref[...]
    src_ref[CONV_HALO + tm:, :] = jnp.where(pos < tiles_per_seq - 1, next_ref[...], 0.0)
    xn = _rms(src_ref[...], g_ref[...]).astype(BF16)
    gathered = _dot(gather_ref[...], xn).astype(BF16)
    n_conv = tm + SUBLANES * n_sub
    lhs_conv = gathered[0:n_conv]
    lhs = gathered[n_conv:]

    z_ref[...] = _dot(lhs, wz_ref[...])
    raw = _dot(lhs, wdt_ref[...]) + dtb_ref[...]
    dt_ref[...] = jnp.maximum(raw, 0.0) + jnp.log1p(jnp.exp(-jnp.abs(raw)))

    n_chunks = wx_ref.shape[0]
    cw = wx_ref.shape[2]
    n_xs = xs_ref.shape[1] // cw
    n_wrap = SSD_CONV - 1
    last_sublane = lax.broadcasted_iota(jnp.int32, (SUBLANES, LANES), 0) == SUBLANES - 1
    for c in range(n_chunks):
        proj = _dot(lhs_conv, wx_ref[c])
        for ci in range(n_sub):
            for lt in range(cw // LANES):
                col = c * cw + lt * LANES
                cur = proj[ci * q:(ci + 1) * q, lt * LANES:(lt + 1) * LANES]
                tail = proj[tm + SUBLANES * ci:tm + SUBLANES * (ci + 1), lt * LANES:(lt + 1) * LANES]
                wrap = []
                for m in range(n_wrap):
                    up = pltpu.roll(cur[SUBLANES * m:SUBLANES * (m + 1)], SUBLANES - 1, axis=0)
                    end = pltpu.roll(tail, (SUBLANES - 1 - m) % SUBLANES, axis=0)
                    wrap.append(jnp.where(last_sublane, end, up))
                taps = jnp.concatenate([cur] + wrap, axis=0)
                acc = cb_ref[:, col:col + LANES] + taps[0:q] * cw_ref[0:1, col:col + LANES]
                for j in range(1, SSD_CONV):
                    acc = acc + taps[SUBLANES * j:SUBLANES * j + q] * cw_ref[j:j + 1, col:col + LANES]
                out = acc * _sigmoid(acc)
                if c < n_xs:
                    xs_ref[ci * q:(ci + 1) * q, col:col + LANES] = out
                else:
                    bc_ref[ci * q:(ci + 1) * q, col - n_xs * cw:col - n_xs * cw + LANES] = out.astype(BF16)


def _ssd_in_gather_matrix(tm):
    q = SSD_CHUNK
    half = SSD_CONV // 2
    times = _chunk_times()
    starts = [CONV_HALO + ci * q for ci in range(tm // q)]
    src = [s + t - half for s in starts for t in times]
    src += [s + q + m - half for s in starts for m in range(SUBLANES)]
    src += [s + t for s in starts for t in times]
    mat = np.zeros((len(src), tm + 2 * CONV_HALO), np.float32)
    mat[np.arange(len(src)), src] = 1.0
    return jnp.asarray(mat, BF16)


def _ssd_in(h, seqlen, norm_g, w_in, dt_bias, conv_w, conv_b, d_inner, conv_dim, tm=256, cw=512):
    t, d = h.shape
    n_dt = w_in.shape[1] - d_inner - conv_dim
    wz = w_in[:, :d_inner].astype(BF16)
    wx = w_in[:, d_inner:d_inner + conv_dim].reshape(d, conv_dim // cw, cw).transpose(1, 0, 2).astype(BF16)
    wdt = jnp.pad(w_in[:, d_inner + conv_dim:], ((0, 0), (0, LANES - n_dt))).astype(BF16)
    dtb = jnp.pad(dt_bias.reshape(1, n_dt), ((0, 0), (0, LANES - n_dt)))
    cwt = jnp.pad(conv_w, ((0, SUBLANES - SSD_CONV), (0, 0)))
    gather = _ssd_in_gather_matrix(tm)
    hb = tm // CONV_HALO
    n_hblk = t // CONV_HALO
    row = lambda i: (i, 0)
    return pl.pallas_call(
        functools.partial(_ssd_in_kernel, tiles_per_seq=seqlen // tm),
        grid=(t // tm,),
        in_specs=[pl.BlockSpec((tm, d), row),
                  pl.BlockSpec((CONV_HALO, d), lambda i: (jnp.maximum(i * hb - 1, 0), 0)),
                  pl.BlockSpec((CONV_HALO, d), lambda i: (jnp.minimum((i + 1) * hb, n_hblk - 1), 0)),
                  _const_spec((1, d)), _const_spec(gather.shape), _const_spec(wz.shape), _const_spec(wx.shape),
                  _const_spec(wdt.shape), _const_spec((1, LANES)), _const_spec(cwt.shape),
                  _const_spec((1, conv_dim))],
        out_specs=[pl.BlockSpec((tm, d_inner), row), pl.BlockSpec((tm, d_inner), row),
                   pl.BlockSpec((tm, conv_dim - d_inner), row), pl.BlockSpec((tm, LANES), row)],
        out_shape=[jax.ShapeDtypeStruct((t, d_inner), F32), jax.ShapeDtypeStruct((t, d_inner), F32),
                   jax.ShapeDtypeStruct((t, conv_dim - d_inner), BF16), jax.ShapeDtypeStruct((t, LANES), F32)],
        scratch_shapes=[pltpu.VMEM((tm + 2 * CONV_HALO, d), F32)],
        compiler_params=_params("parallel"),
        name="ssd_in",
    )(h, h, h, norm_g.reshape(1, d), gather, wz, wx, wdt, dtb, cwt, conv_b.reshape(1, conv_dim))


def _ssd_scan_kernel(xs_ref, b_ref, c_ref, dt_ref, a_ref, tri_ref, extra_ref, dsk_ref, y_ref, st_ref,
                     *, reverse, col0):
    q = SSD_CHUNK
    hg = SSD_HEADS // SSD_GROUPS
    gw = hg * SSD_HEAD_DIM
    n_sub = xs_ref.shape[1] // q

    @pl.when(pl.program_id(1) == 0)
    def _():
        st_ref[...] = jnp.zeros_like(st_ref)

    def time_of(idx):
        return (q // SUBLANES) * (idx & (SUBLANES - 1)) + (idx >> 3)

    t_row = time_of(lax.broadcasted_iota(jnp.int32, (q, q), 0))
    t_col = time_of(lax.broadcasted_iota(jnp.int32, (q, q), 1))
    causal = (t_row >= t_col) if not reverse else (t_row <= t_col)
    lane_head = lax.broadcasted_iota(jnp.int32, (q, gw), 1) // SSD_HEAD_DIM
    low_half = lax.broadcasted_iota(jnp.int32, (q, LANES), 1) < SSD_HEAD_DIM
    a = a_ref[...]
    tri = tri_ref[...]
    edge_row = q - 1 if not reverse else 0

    for ci in (range(n_sub) if not reverse else range(n_sub - 1, -1, -1)):
        rows = slice(ci * q, (ci + 1) * q)
        dt = dt_ref[0, rows, :]
        hi, mid, lo = _split3(dt * a)
        cum = _dot(tri, hi) + _dot(tri, mid) + _dot(tri, lo)
        dt_dec = dt * jnp.exp(cum[edge_row:edge_row + 1, :] - cum)
        cum_t = cum.T
        dt_t = dt.T
        dd_t = dt_dec.T
        xs = xs_ref[0, rows, :]
        xs_b = xs.astype(BF16)

        for g in range(SSD_GROUPS):
            bg = b_ref[0, rows, g * SSD_D_STATE:(g + 1) * SSD_D_STATE]
            cg = c_ref[0, rows, g * SSD_D_STATE:(g + 1) * SSD_D_STATE]
            cb = lax.dot_general(cg, bg, (((1,), (1,)), ((), ())), preferred_element_type=F32)
            bg_t = bg.astype(F32).T
            scores, b_scaled, ecum = [], [], []
            for j in range(hg):
                col = col0 + g * hg + j
                cum_q = jnp.broadcast_to(cum[:, col:col + 1], (q, q))
                decay = jnp.exp(jnp.where(causal, cum_q - cum_t[col:col + 1, :], -jnp.inf))
                scores.append((cb * decay * dt_t[col:col + 1, :]).astype(BF16))
                b_scaled.append((bg_t * dd_t[col:col + 1, :]).astype(BF16))
                ecum.append(jnp.exp(cum_q))
            xg = xs_b[:, g * gw:(g + 1) * gw]
            x_blocks = jnp.concatenate([jnp.where(lane_head == j, xg, jnp.zeros_like(xg)) for j in range(hg)],
                                       axis=0)
            y_diag = _dot(jnp.concatenate(scores, axis=1), x_blocks)
            ecum_g = jnp.concatenate([jnp.where(low_half, ecum[2 * i], ecum[2 * i + 1]) for i in range(hg // 2)],
                                     axis=1)
            st = st_ref[g]
            y = y_diag + _dot(cg, st.astype(BF16)) * ecum_g
            if extra_ref is not None:
                y = y + extra_ref[0, rows, g * gw:(g + 1) * gw]
            if dsk_ref is not None:
                y = y + xs[:, g * gw:(g + 1) * gw] * dsk_ref[:, g * gw:(g + 1) * gw]
            y_ref[0, rows, g * gw:(g + 1) * gw] = y
            new = _dot(jnp.concatenate(b_scaled, axis=1), x_blocks)
            st_ref[g] = st * ecum_g[edge_row:edge_row + 1, :] + new


def _scan_fwd_kernel(xs_ref, b_ref, c_ref, dt_ref, a_ref, tri_ref, dsk_ref, y_ref, st_ref):
    _ssd_scan_kernel(xs_ref, b_ref, c_ref, dt_ref, a_ref, tri_ref, None, dsk_ref, y_ref, st_ref,
                     reverse=False, col0=0)


def _scan_bwd_kernel(xs_ref, b_ref, c_ref, dt_ref, a_ref, tri_ref, yf_ref, y_ref, st_ref):
    _ssd_scan_kernel(xs_ref, b_ref, c_ref, dt_ref, a_ref, tri_ref, yf_ref, None, y_ref, st_ref,
                     reverse=True, col0=SSD_HEADS)


def _ssd_scan(xs, bc, dt, a_log, d_skip):
    bsz, seqlen, d_inner = xs.shape
    rows = SSD_CHUNK * min(SCAN_CHUNKS_PER_STEP, seqlen // SSD_CHUNK)
    nb = seqlen // rows
    bcw = SSD_GROUPS * SSD_D_STATE
    a = jnp.pad(-jnp.exp(a_log.astype(F32)).reshape(1, 2 * SSD_HEADS), ((0, 0), (0, LANES - 2 * SSD_HEADS)))
    dsk = jnp.repeat(d_skip.astype(F32), SSD_HEAD_DIM).reshape(1, d_inner)
    times = _chunk_times()
    prefix = (times[None, :] <= times[:, None]).astype(np.float32)
    st_shape = pltpu.VMEM((SSD_GROUPS, SSD_D_STATE, d_inner // SSD_GROUPS), F32)

    def specs(bidx, b_blk, c_blk):
        return [
            pl.BlockSpec((1, rows, d_inner), lambda bi, i: (bi, bidx(i), 0)),
            pl.BlockSpec((1, rows, bcw), lambda bi, i: (bi, bidx(i), b_blk)),
            pl.BlockSpec((1, rows, bcw), lambda bi, i: (bi, bidx(i), c_blk)),
            pl.BlockSpec((1, rows, LANES), lambda bi, i: (bi, bidx(i), 0)),
            _const_spec((1, LANES)), _const_spec((SSD_CHUNK, SSD_CHUNK)),
        ]

    fwd = lambda i: i
    y_f = pl.pallas_call(
        _scan_fwd_kernel,
        grid=(bsz, nb),
        in_specs=specs(fwd, 0, 1) + [_const_spec((1, d_inner))],
        out_specs=pl.BlockSpec((1, rows, d_inner), lambda bi, i: (bi, i, 0)),
        out_shape=jax.ShapeDtypeStruct((bsz, seqlen, d_inner), F32),
        scratch_shapes=[st_shape],
        compiler_params=_params("parallel", "arbitrary"),
        name="ssd_scan_fwd",
    )(xs, bc, bc, dt, a, jnp.asarray(prefix, BF16), dsk)
    bwd = lambda i: nb - 1 - i
    return pl.pallas_call(
        _scan_bwd_kernel,
        grid=(bsz, nb),
        in_specs=specs(bwd, 2, 3) + [pl.BlockSpec((1, rows, d_inner), lambda bi, i: (bi, bwd(i), 0))],
        out_specs=pl.BlockSpec((1, rows, d_inner), lambda bi, i: (bi, bwd(i), 0)),
        out_shape=jax.ShapeDtypeStruct((bsz, seqlen, d_inner), F32),
        scratch_shapes=[st_shape],
        compiler_params=_params("parallel", "arbitrary"),
        name="ssd_scan_bwd",
    )(xs, bc, bc, dt, a, jnp.asarray(prefix.T, BF16), y_f)


def _ssd_out_kernel(h_ref, y_ref, z_ref, g_ref, order_ref, w_ref, o_ref):
    q = SSD_CHUNK
    z = z_ref[...]
    y = _rms(y_ref[...] * (z * _sigmoid(z)), g_ref[...]).astype(BF16)
    y_time = [_dot(order_ref[...], y[ci * q:(ci + 1) * q]).astype(BF16) for ci in range(y.shape[0] // q)]
    o_ref[...] = h_ref[...] + _dot(jnp.concatenate(y_time, axis=0), w_ref[...])


def _ssd_out(h, y, z, norm_g, w_out, tm=512):
    t, d = h.shape
    d_inner = y.shape[1]
    to_time_order = (_chunk_times()[None, :] == np.arange(SSD_CHUNK)[:, None]).astype(np.float32)
    row = lambda i: (i, 0)
    return pl.pallas_call(
        _ssd_out_kernel,
        grid=(t // tm,),
        in_specs=[pl.BlockSpec((tm, d), row), pl.BlockSpec((tm, d_inner), row), pl.BlockSpec((tm, d_inner), row),
                  _const_spec((1, d_inner)), _const_spec((SSD_CHUNK, SSD_CHUNK)), _const_spec((d_inner, d))],
        out_specs=pl.BlockSpec((tm, d), row),
        out_shape=jax.ShapeDtypeStruct((t, d), F32),
        compiler_params=_params("parallel"),
        name="ssd_out",
    )(h, y, z, norm_g.reshape(1, d_inner), jnp.asarray(to_time_order, BF16), w_out.astype(BF16))


def _ssd_mixer(h, bsz, mix_g, w_in, conv_w, conv_b, dt_bias, a_log, d_skip, norm_g, w_out):
    t, d = h.shape
    seqlen = t // bsz
    d_inner = SSD_HEADS * SSD_HEAD_DIM
    bcw = SSD_GROUPS * SSD_D_STATE
    conv_dim = d_inner + 4 * bcw
    z, xs, bc, dt = _ssd_in(h, seqlen, mix_g, w_in, dt_bias, conv_w, conv_b, d_inner, conv_dim)
    y = _ssd_scan(xs.reshape(bsz, seqlen, d_inner), bc.reshape(bsz, seqlen, 4 * bcw),
                  dt.reshape(bsz, seqlen, LANES), a_log, d_skip)
    return _ssd_out(h, y.reshape(t, d_inner), z, norm_g, w_out)


def _na_qkv_kernel(h_ref, g_ref, w_ref, avg_ref, qg_ref, kg_ref, q_ref, k_ref, v_ref, raw_ref, ms_ref):
    xn = _rms(h_ref[...], g_ref[...]).astype(BF16)
    d = q_ref.shape[1]
    gw = avg_ref.shape[0]
    n_groups = 2 * d // gw
    for c in range(n_groups):
        raw_ref[:, c * gw:(c + 1) * gw] = _dot(xn, w_ref[:, c * gw:(c + 1) * gw])
    v_ref[...] = _dot(xn, w_ref[:, 2 * d:3 * d]).astype(BF16)
    for c in range(n_groups):
        x = raw_ref[:, c * gw:(c + 1) * gw]
        hi, lo = _split2(x * x)
        ms_ref[:, c * gw:(c + 1) * gw] = _dot(hi, avg_ref[...]) + _dot(lo, avg_ref[...])
    for c in range(n_groups):
        cols = slice(c * gw, (c + 1) * gw)
        x = raw_ref[:, cols] * lax.rsqrt(ms_ref[:, cols] + RMS_EPS)
        if c < n_groups // 2:
            q_ref[:, cols] = (x * qg_ref[:, cols] * (NA_HEAD_DIM ** -0.5 * LOG2E)).astype(BF16)
        else:
            kcols = slice(c * gw - d, (c + 1) * gw - d)
            k_ref[:, kcols] = (x * kg_ref[:, kcols]).astype(BF16)


def _na_qkv(h, norm_g, w_qkv, q_norm, k_norm, tm=512, gw=256):
    t, d = h.shape
    n_heads = d // NA_HEAD_DIM
    same_head = np.arange(gw)[:, None] // NA_HEAD_DIM == np.arange(gw)[None, :] // NA_HEAD_DIM
    avg = jnp.asarray(same_head.astype(np.float32) / NA_HEAD_DIM, BF16)
    row = lambda i: (i, 0)
    out = jax.ShapeDtypeStruct((t, d), BF16)
    return pl.pallas_call(
        _na_qkv_kernel,
        grid=(t // tm,),
        in_specs=[pl.BlockSpec((tm, d), row), _const_spec((1, d)), _const_spec((d, 3 * d)), _const_spec((gw, gw)),
                  _const_spec((1, d)), _const_spec((1, d))],
        out_specs=[pl.BlockSpec((tm, d), row)] * 3,
        out_shape=[out, out, out],
        scratch_shapes=[pltpu.VMEM((tm, 2 * d), F32), pltpu.VMEM((tm, 2 * d), F32)],
        compiler_params=_params("parallel"),
        name="na_qkv",
    )(h, norm_g.reshape(1, d), w_qkv.astype(BF16), avg,
      jnp.tile(q_norm, n_heads).reshape(1, d), jnp.tile(k_norm, n_heads).reshape(1, d))


def _na_attn_kernel(q_ref, k_ref, v_ref, *rest, rows):
    bias_refs = rest[:NA_ROWS_PER_STEP]
    o_ref, s_ref, p_ref = rest[NA_ROWS_PER_STEP:]
    nk = NA_WIN_ROWS * GRID_W
    n_pairs = q_ref.shape[2] // LANES
    low = lax.broadcasted_iota(jnp.int32, (GRID_W, LANES), 1) < NA_HEAD_DIM
    starts = []
    for i in range(NA_ROWS_PER_STEP):
        r = pl.program_id(1) * NA_ROWS_PER_STEP + i
        r0 = jnp.clip(r - NA_WIN_ROWS // 2, 0, rows - NA_WIN_ROWS)
        starts.append(pl.multiple_of(r0 * GRID_W, GRID_W))
    units = [(i, hp) for i in range(NA_ROWS_PER_STEP) for hp in range(n_pairs)]

    row_max = []
    for u, (i, hp) in enumerate(units):
        q2 = q_ref[0, i * GRID_W:(i + 1) * GRID_W, hp * LANES:(hp + 1) * LANES]
        zero = jnp.zeros_like(q2)
        qab = jnp.concatenate([jnp.where(low, q2, zero), jnp.where(low, zero, q2)], axis=0)
        k2 = k_ref[0, pl.ds(starts[i], nk), hp * LANES:(hp + 1) * LANES]
        s = lax.dot_general(qab, k2, (((1,), (1,)), ((), ())), preferred_element_type=F32)
        s = s + bias_refs[i][0, hp]
        s_ref[u] = s
        row_max.append(jnp.max(s, axis=-1, keepdims=True))
    inv_sum = []
    for u in range(len(units)):
        e = jnp.exp2(s_ref[u] - row_max[u])
        inv_sum.append(1.0 / jnp.sum(e, axis=-1, keepdims=True))
        p_ref[u] = e.astype(BF16)
    for u, (i, hp) in enumerate(units):
        v2 = v_ref[0, pl.ds(starts[i], nk), hp * LANES:(hp + 1) * LANES]
        o2 = _dot(p_ref[u], v2) * inv_sum[u]
        o_ref[0, i * GRID_W:(i + 1) * GRID_W, hp * LANES:(hp + 1) * LANES] = (
            jnp.where(low, o2[:GRID_W], o2[GRID_W:]).astype(o_ref.dtype))


def _na_bias_table(rpb):
    n_heads = rpb.shape[0]
    qc = np.arange(GRID_W)[:, None]
    kc = np.arange(GRID_W)[None, :]
    win_c0 = np.clip(qc - NA_WIN_COLS // 2, 0, GRID_W - NA_WIN_COLS)
    valid = (kc >= win_c0) & (kc < win_c0 + NA_WIN_COLS)
    col_rel = np.clip(kc - qc + NA_WIN_COLS - 1, 0, 2 * NA_WIN_COLS - 2)
    row_rel = np.arange(NA_WIN_ROWS)[:, None] + np.arange(NA_WIN_ROWS)[None, :]
    tab = (rpb.astype(F32) * LOG2E)[:, row_rel][:, :, :, col_rel]
    tab = jnp.where(jnp.asarray(valid), tab, MASK_VALUE)
    tab = tab.transpose(1, 0, 3, 2, 4).reshape(NA_WIN_ROWS, n_heads // 2, 2 * GRID_W, NA_WIN_ROWS * GRID_W)
    return tab


def _na_attn(q, k, v, rpb):
    bsz, seqlen, d = q.shape
    rows = seqlen // GRID_W
    assert rows >= NA_WIN_ROWS and rows % NA_ROWS_PER_STEP == 0
    bias = _na_bias_table(rpb)
    n_units = NA_ROWS_PER_STEP * (d // LANES)
    blk = NA_ROWS_PER_STEP * GRID_W

    def bias_spec(i):
        def idx(bi, step):
            r = step * NA_ROWS_PER_STEP + i
            r0 = jnp.clip(r - NA_WIN_ROWS // 2, 0, rows - NA_WIN_ROWS)
            return (r0 - r + NA_WIN_ROWS - 1, 0, 0, 0)
        return pl.BlockSpec((1,) + bias.shape[1:], idx)

    return pl.pallas_call(
        functools.partial(_na_attn_kernel, rows=rows),
        grid=(bsz, rows // NA_ROWS_PER_STEP),
        in_specs=[pl.BlockSpec((1, blk, d), lambda bi, r: (bi, r, 0)),
                  pl.BlockSpec((1, seqlen, d), lambda bi, r: (bi, 0, 0)),
                  pl.BlockSpec((1, seqlen, d), lambda bi, r: (bi, 0, 0))]
                 + [bias_spec(i) for i in range(NA_ROWS_PER_STEP)],
        out_specs=pl.BlockSpec((1, blk, d), lambda bi, r: (bi, r, 0)),
        out_shape=jax.ShapeDtypeStruct((bsz, seqlen, d), BF16),
        scratch_shapes=[pltpu.VMEM((n_units, 2 * GRID_W, NA_WIN_ROWS * GRID_W), F32),
                        pltpu.VMEM((n_units, 2 * GRID_W, NA_WIN_ROWS * GRID_W), BF16)],
        compiler_params=_params("parallel", "arbitrary"),
        name="na_attn",
    )(q, k, v, *([bias] * NA_ROWS_PER_STEP))


def _proj_residual_kernel(h_ref, x_ref, w_ref, o_ref):
    o_ref[...] = h_ref[...] + _dot(x_ref[...], w_ref[...])


def _proj_residual(h, x, w, tm=512):
    t, d = h.shape
    row = lambda i: (i, 0)
    return pl.pallas_call(
        _proj_residual_kernel,
        grid=(t // tm,),
        in_specs=[pl.BlockSpec((tm, d), row), pl.BlockSpec((tm, x.shape[1]), row), _const_spec(w.shape)],
        out_specs=pl.BlockSpec((tm, d), row),
        out_shape=jax.ShapeDtypeStruct((t, d), F32),
        compiler_params=_params("parallel"),
        name="proj_residual",
    )(h, x, w.astype(BF16))


def _na_mixer(h, bsz, mix_g, w_qkv, q_norm, k_norm, rpb, w_out):
    t, d = h.shape
    seqlen = t // bsz
    q, k, v = _na_qkv(h, mix_g, w_qkv, q_norm, k_norm)
    shp = (bsz, seqlen, d)
    o = _na_attn(q.reshape(shp), k.reshape(shp), v.reshape(shp), rpb)
    return _proj_residual(h, o.reshape(t, d), w_out)


def kernel(x, p, ffn1_norm, ffn1_w_gu, ffn1_w_down, mix_norm, ffn2_norm, ffn2_w_gu, ffn2_w_down, ple_norm, ple_w_gate, ple_w_proj, ple_post_norm, ssd_w_in, ssd_conv_w, ssd_conv_b, ssd_dt_bias, ssd_a_log, ssd_d, ssd_norm, ssd_w_out, na_w_qkv, na_q_norm, na_k_norm, na_rpb, na_w_out):
    bsz, seqlen, d = x.shape
    depth = p.shape[0]
    t = bsz * seqlen
    h = x.reshape(t, d)
    p_all = p.reshape(depth, t, p.shape[-1])
    for i in range(depth):
        h = _ffn(h, ffn1_norm[i], ffn1_w_gu[i], ffn1_w_down[i])
        j = i // 2
        if i % 2 == 0:
            h = _ssd_mixer(h, bsz, mix_norm[i], ssd_w_in[j], ssd_conv_w[j], ssd_conv_b[j], ssd_dt_bias[j],
                           ssd_a_log[j], ssd_d[j], ssd_norm[j], ssd_w_out[j])
        else:
            h = _na_mixer(h, bsz, mix_norm[i], na_w_qkv[j], na_q_norm[j], na_k_norm[j], na_rpb[j], na_w_out[j])
        h = _ffn(h, ffn2_norm[i], ffn2_w_gu[i], ffn2_w_down[i],
                 ple=(p_all, i, ple_norm[i], ple_w_gate[i], ple_w_proj[i], ple_post_norm[i]))
    return h.reshape(bsz, seqlen, d)
```

```python
import functools
import math

import jax
import jax.numpy as jnp
import numpy as np
from jax import lax
from jax.experimental import pallas as pl
from jax.experimental.pallas import tpu as pltpu

F32 = jnp.float32
BF16 = jnp.bfloat16

RMS_EPS = 1e-6
SSD_HEAD_DIM = 64
SSD_HEADS = 32
SSD_GROUPS = 8
SSD_D_STATE = 128
SSD_CONV = 5
SSD_CHUNK = 128
GRID_W = 64
NA_HEAD_DIM = 64
NA_WIN_ROWS = 8
NA_WIN_COLS = 16
NA_ROWS_PER_STEP = 2
FFN_CHUNK = 256
SUBLANES = 8
LANES = 128
CONV_HALO = SUBLANES
SCAN_CHUNKS_PER_STEP = 2
VMEM_LIMIT = 56 * 1024 * 1024
MASK_VALUE = -1e30
LOG2E = math.log2(math.e)


def _params(*sem):
    return pltpu.CompilerParams(dimension_semantics=sem, vmem_limit_bytes=VMEM_LIMIT)


def _const_spec(shape):
    nd = len(shape)
    return pl.BlockSpec(shape, lambda *_: (0,) * nd, pipeline_mode=pl.Buffered(1))


def _rms(x, g):
    ms = jnp.mean(x * x, axis=-1, keepdims=True)
    return x * lax.rsqrt(ms + RMS_EPS) * g


def _sigmoid(x):
    return 1.0 / (1.0 + jnp.exp(-x))


def _dot(a, b):
    return jnp.dot(a, b, preferred_element_type=F32)


def _split2(x):
    hi = x.astype(BF16)
    lo = (x - hi.astype(F32)).astype(BF16)
    return hi, lo


def _split3(x):
    hi = x.astype(BF16)
    r = x - hi.astype(F32)
    mid = r.astype(BF16)
    lo = (r - mid.astype(F32)).astype(BF16)
    return hi, mid, lo


def _chunk_times():
    p = np.arange(SSD_CHUNK)
    return (SSD_CHUNK // SUBLANES) * (p % SUBLANES) + p // SUBLANES


def _ffn_body(h_ref, g_ref, wgu_ref, wd_ref, a_ref):
    x = h_ref[...]
    xn = _rms(x, g_ref[...]).astype(BF16)
    n_chunks = wgu_ref.shape[0]
    for c in range(n_chunks):
        gu = _dot(xn, wgu_ref[c])
        gate = gu[:, :FFN_CHUNK]
        up = gu[:, FFN_CHUNK:]
        a_ref[:, c * FFN_CHUNK:(c + 1) * FFN_CHUNK] = (gate * _sigmoid(gate) * up).astype(BF16)
    return x + 0.5 * _dot(a_ref[...], wd_ref[...])


def _ffn_kernel(h_ref, g_ref, wgu_ref, wd_ref, o_ref, a_ref):
    o_ref[...] = _ffn_body(h_ref, g_ref, wgu_ref, wd_ref, a_ref)


def _ffn_ple_kernel(h_ref, g_ref, wgu_ref, wd_ref, p_ref, pg_ref, wgate_ref, wproj_ref, ppg_ref, o_ref, a_ref):
    h2 = _ffn_body(h_ref, g_ref, wgu_ref, wd_ref, a_ref)
    gate = _sigmoid(_dot(_rms(h2, pg_ref[...]).astype(BF16), wgate_ref[...]))
    emb = _dot(p_ref[...].astype(BF16), wproj_ref[...])
    o_ref[...] = h2 + gate * _rms(emb, ppg_ref[...])


def _prep_ffn_weights(w_gu, w_down):
    d_model, two_ff = w_gu.shape
    d_ff = two_ff // 2
    n_chunks = d_ff // FFN_CHUNK
    wg = w_gu[:, :d_ff].reshape(d_model, n_chunks, FFN_CHUNK)
    wu = w_gu[:, d_ff:].reshape(d_model, n_chunks, FFN_CHUNK)
    wgu = jnp.concatenate([wg, wu], axis=-1).transpose(1, 0, 2).astype(BF16)
    return wgu, w_down.astype(BF16)


def _ffn(h, norm_g, w_gu, w_down, ple=None, tm=512):
    t, d = h.shape
    wgu, wd = _prep_ffn_weights(w_gu, w_down)
    d_ff = wd.shape[0]
    row = lambda i: (i, 0)
    in_specs = [pl.BlockSpec((tm, d), row), _const_spec((1, d)), _const_spec(wgu.shape), _const_spec(wd.shape)]
    args = [h, norm_g.reshape(1, d), wgu, wd]
    kern = _ffn_kernel
    if ple is not None:
        p_all, layer, ple_g, w_gate, w_proj, post_g = ple
        dp = p_all.shape[-1]
        in_specs += [pl.BlockSpec((pl.Squeezed(), tm, dp), lambda i: (layer, i, 0)), _const_spec((1, d)),
                     _const_spec((d, d)), _const_spec((dp, d)), _const_spec((1, d))]
        args += [p_all, ple_g.reshape(1, d), w_gate.astype(BF16), w_proj.astype(BF16), post_g.reshape(1, d)]
        kern = _ffn_ple_kernel
    return pl.pallas_call(
        kern,
        grid=(t // tm,),
        in_specs=in_specs,
        out_specs=pl.BlockSpec((tm, d), row),
        out_shape=jax.ShapeDtypeStruct((t, d), F32),
        scratch_shapes=[pltpu.VMEM((tm, d_ff), BF16)],
        compiler_params=_params("parallel"),
        name="ffn_ple" if ple is not None else "ffn",
    )(*args)


def _ssd_in_kernel(h_ref, prev_ref, next_ref, g_ref, gather_ref, tail_ref, wz_ref, wx_ref, wdt_ref, dtb_ref,
                   cw_ref, cb_ref, z_ref, xs_ref, bc_ref, dt_ref, src_ref, proj_ref, *, tiles_per_seq):
    tm = h_ref.shape[0]
    q = SSD_CHUNK
    n_sub = tm // q
    pos = lax.rem(pl.program_id(0), tiles_per_seq)

    src_ref[0:CONV_HALO, :] = jnp.where(pos > 0, prev_ref[...], 0.0)
    src_ref[CONV_HALO:CONV_HALO + tm, :] = h_ref[...]
    src_ref[CONV_HALO + tm:, :] = jnp.where(pos < tiles_per_seq - 1, next_ref[...], 0.0)
    xn = _rms(src_ref[...], g_ref[...]).astype(BF16)
    conv_rows, plain_rows = [], []
    for ci in range(n_sub):
        picked = _dot(gather_ref[...], xn[ci * q:(ci + 1) * q + 2 * CONV_HALO]).astype(BF16)
        conv_rows.append(picked[0:q])
        plain_rows.append(picked[q:])
    tails = _dot(tail_ref[...], xn).astype(BF16)
    lhs_conv = jnp.concatenate(conv_rows + [tails], axis=0)
    lhs = jnp.concatenate(plain_rows, axis=0)

    n_chunks = wx_ref.shape[0]
    cw = wx_ref.shape[2]
    n_xs = xs_ref.shape[1] // cw
    n_wrap = SSD_CONV - 1
    z_every = n_chunks * cw // z_ref.shape[1]
    last_sublane = lax.broadcasted_iota(jnp.int32, (SUBLANES, LANES), 0) == SUBLANES - 1
    proj_ref[0] = _dot(lhs_conv, wx_ref[0])
    for c in range(n_chunks):
        if c + 1 < n_chunks:
            proj_ref[(c + 1) % 2] = _dot(lhs_conv, wx_ref[c + 1])
        if c % z_every == z_every - 1:
            zc = (c // z_every) * cw
            z_ref[:, zc:zc + cw] = _dot(lhs, wz_ref[:, zc:zc + cw])
        for ci in range(n_sub):
            for lt in range(cw // LANES):
                col = c * cw + lt * LANES
                cur = proj_ref[c % 2, ci * q:(ci + 1) * q, lt * LANES:(lt + 1) * LANES]
                tail = proj_ref[c % 2, tm + SUBLANES * ci:tm + SUBLANES * (ci + 1), lt * LANES:(lt + 1) * LANES]
                wrap = []
                for m in range(n_wrap):
                    up = pltpu.roll(cur[SUBLANES * m:SUBLANES * (m + 1)], SUBLANES - 1, axis=0)
                    end = pltpu.roll(tail, (SUBLANES - 1 - m) % SUBLANES, axis=0)
                    wrap.append(jnp.where(last_sublane, end, up))
                taps = jnp.concatenate([cur] + wrap, axis=0)
                acc = cb_ref[:, col:col + LANES] + taps[0:q] * cw_ref[0:1, col:col + LANES]
                for j in range(1, SSD_CONV):
                    acc = acc + taps[SUBLANES * j:SUBLANES * j + q] * cw_ref[j:j + 1, col:col + LANES]
                out = acc * _sigmoid(acc)
                if c < n_xs:
                    xs_ref[ci * q:(ci + 1) * q, col:col + LANES] = out.astype(BF16)
                else:
                    bc_ref[ci * q:(ci + 1) * q, col - n_xs * cw:col - n_xs * cw + LANES] = out.astype(BF16)
    raw = _dot(lhs, wdt_ref[...]) + dtb_ref[...]
    dt_ref[...] = jnp.maximum(raw, 0.0) + jnp.log1p(jnp.exp(-jnp.abs(raw)))


def _one_hot_rows(cols, width):
    mat = np.zeros((len(cols), width), np.float32)
    mat[np.arange(len(cols)), cols] = 1.0
    return jnp.asarray(mat, BF16)


def _ssd_in_gather_matrices(tm):
    q = SSD_CHUNK
    half = SSD_CONV // 2
    times = _chunk_times()
    chunk = [CONV_HALO + t - half for t in times] + [CONV_HALO + t for t in times]
    tails = [CONV_HALO + ci * q + q + m - half for ci in range(tm // q) for m in range(SUBLANES)]
    return _one_hot_rows(chunk, q + 2 * CONV_HALO), _one_hot_rows(tails, tm + 2 * CONV_HALO)


def _ssd_in(h, seqlen, norm_g, w_in, dt_bias, conv_w, conv_b, d_inner, conv_dim, tm=512, cw=512):
    t, d = h.shape
    n_dt = w_in.shape[1] - d_inner - conv_dim
    wz = w_in[:, :d_inner].astype(BF16)
    wx = w_in[:, d_inner:d_inner + conv_dim].reshape(d, conv_dim // cw, cw).transpose(1, 0, 2).astype(BF16)
    wdt = jnp.pad(w_in[:, d_inner + conv_dim:], ((0, 0), (0, LANES - n_dt))).astype(BF16)
    dtb = jnp.pad(dt_bias.reshape(1, n_dt), ((0, 0), (0, LANES - n_dt)))
    cwt = jnp.pad(conv_w, ((0, SUBLANES - SSD_CONV), (0, 0)))
    assert (tm // SSD_CHUNK) % 2 == 0
    gather, tail = _ssd_in_gather_matrices(tm)
    hb = tm // CONV_HALO
    n_hblk = t // CONV_HALO
    row = lambda i: (i, 0)
    return pl.pallas_call(
        functools.partial(_ssd_in_kernel, tiles_per_seq=seqlen // tm),
        grid=(t // tm,),
        in_specs=[pl.BlockSpec((tm, d), row),
                  pl.BlockSpec((CONV_HALO, d), lambda i: (jnp.maximum(i * hb - 1, 0), 0)),
                  pl.BlockSpec((CONV_HALO, d), lambda i: (jnp.minimum((i + 1) * hb, n_hblk - 1), 0)),
                  _const_spec((1, d)), _const_spec(gather.shape), _const_spec(tail.shape), _const_spec(wz.shape),
                  _const_spec(wx.shape),
                  _const_spec(wdt.shape), _const_spec((1, LANES)), _const_spec(cwt.shape),
                  _const_spec((1, conv_dim))],
        out_specs=[pl.BlockSpec((tm, d_inner), row), pl.BlockSpec((tm, d_inner), row),
                   pl.BlockSpec((tm, conv_dim - d_inner), row), pl.BlockSpec((tm, LANES), row)],
        out_shape=[jax.ShapeDtypeStruct((t, d_inner), F32), jax.ShapeDtypeStruct((t, d_inner), BF16),
                   jax.ShapeDtypeStruct((t, conv_dim - d_inner), BF16), jax.ShapeDtypeStruct((t, LANES), F32)],
        scratch_shapes=[pltpu.VMEM((tm + 2 * CONV_HALO, d), F32),
                        pltpu.VMEM((2, tm + SUBLANES * (tm // SSD_CHUNK), cw), F32)],
        compiler_params=_params("parallel"),
        name="ssd_in",
    )(h, h, h, norm_g.reshape(1, d), gather, tail, wz, wx, wdt, dtb, cwt, conv_b.reshape(1, conv_dim))


def _scan_decays(dt_ref, rows, a2, tri, edge_row):
    dt = dt_ref[0, rows, :]
    hi, mid, lo = _split3(dt * a2)
    cum = _dot(tri, hi) + _dot(tri, mid) + _dot(tri, lo)
    dt_dec = dt * jnp.exp2(cum[edge_row:edge_row + 1, :] - cum)
    return cum, cum.T, dt.T, dt_dec.T


def _scan_chunk(xs_ref, b_ref, c_ref, y_ref, st_ref, dsk_ref, rows, decays, masks, col0, edge_row):
    q = SSD_CHUNK
    hg = SSD_HEADS // SSD_GROUPS
    gw = hg * SSD_HEAD_DIM
    cum, cum_t, dt_t, dd_t = decays
    causal, lane_head, low_half = masks
    for g in range(SSD_GROUPS):
        bg = b_ref[0, rows, g * SSD_D_STATE:(g + 1) * SSD_D_STATE]
        cg = c_ref[0, rows, g * SSD_D_STATE:(g + 1) * SSD_D_STATE]
        cb = lax.dot_general(cg, bg, (((1,), (1,)), ((), ())), preferred_element_type=F32)
        bg_t = bg.astype(F32).T
        scores, b_scaled, ecum = [], [], []
        for j in range(hg):
            col = col0 + g * hg + j
            cum_q = jnp.broadcast_to(cum[:, col:col + 1], (q, q))
            decay = jnp.exp2(jnp.where(causal, cum_q - cum_t[col:col + 1, :], -jnp.inf))
            scores.append((cb * decay * dt_t[col:col + 1, :]).astype(BF16))
            b_scaled.append((bg_t * dd_t[col:col + 1, :]).astype(BF16))
            ecum.append(jnp.exp2(cum_q))
        xg = xs_ref[0, rows, g * gw:(g + 1) * gw]
        x_blocks = jnp.concatenate([jnp.where(lane_head == j, xg, jnp.zeros_like(xg)) for j in range(hg)],
                                   axis=0)
        y_diag = _dot(jnp.concatenate(scores, axis=1), x_blocks)
        ecum_g = jnp.concatenate([jnp.where(low_half, ecum[2 * i], ecum[2 * i + 1]) for i in range(hg // 2)],
                                 axis=1)
        st = st_ref[g]
        y = y_diag + _dot(cg, st.astype(BF16)) * ecum_g
        if dsk_ref is not None:
            y = y + xg.astype(F32) * dsk_ref[:, g * gw:(g + 1) * gw]
        y_ref[0, rows, g * gw:(g + 1) * gw] = y
        new = _dot(jnp.concatenate(b_scaled, axis=1), x_blocks)
        st_ref[g] = st * ecum_g[edge_row:edge_row + 1, :] + new


def _ssd_scan_kernel(xsf_ref, bf_ref, cf_ref, dtf_ref, xsb_ref, bb_ref, cb_ref, dtb_ref, a_ref, trif_ref, trib_ref,
                     dsk_ref, yf_ref, yb_ref, stf_ref, stb_ref):
    q = SSD_CHUNK
    gw = (SSD_HEADS // SSD_GROUPS) * SSD_HEAD_DIM
    n_sub = xsf_ref.shape[1] // q

    @pl.when(pl.program_id(1) == 0)
    def _():
        stf_ref[...] = jnp.zeros_like(stf_ref)
        stb_ref[...] = jnp.zeros_like(stb_ref)

    def time_of(idx):
        return (q // SUBLANES) * (idx & (SUBLANES - 1)) + (idx >> 3)

    t_row = time_of(lax.broadcasted_iota(jnp.int32, (q, q), 0))
    t_col = time_of(lax.broadcasted_iota(jnp.int32, (q, q), 1))
    lane_head = lax.broadcasted_iota(jnp.int32, (q, gw), 1) // SSD_HEAD_DIM
    low_half = lax.broadcasted_iota(jnp.int32, (q, LANES), 1) < SSD_HEAD_DIM
    masks_f = (t_row >= t_col, lane_head, low_half)
    masks_b = (t_row <= t_col, lane_head, low_half)
    a2 = a_ref[...]
    order_f = [slice(ci * q, (ci + 1) * q) for ci in range(n_sub)]
    order_b = order_f[::-1]
    dec_f = [_scan_decays(dtf_ref, rows, a2, trif_ref[...], q - 1) for rows in order_f]
    dec_b = [_scan_decays(dtb_ref, rows, a2, trib_ref[...], 0) for rows in order_b]
    for i in range(n_sub):
        _scan_chunk(xsf_ref, bf_ref, cf_ref, yf_ref, stf_ref, dsk_ref, order_f[i], dec_f[i], masks_f, 0, q - 1)
        _scan_chunk(xsb_ref, bb_ref, cb_ref, yb_ref, stb_ref, None, order_b[i], dec_b[i], masks_b, SSD_HEADS, 0)


def _ssd_scan(xs, bc, dt, a_log, d_skip):
    bsz, seqlen, d_inner = xs.shape
    rows = SSD_CHUNK * min(SCAN_CHUNKS_PER_STEP, seqlen // SSD_CHUNK)
    nb = seqlen // rows
    bcw = SSD_GROUPS * SSD_D_STATE
    a2 = jnp.pad(-jnp.exp(a_log.astype(F32)).reshape(1, 2 * SSD_HEADS) * LOG2E,
                 ((0, 0), (0, LANES - 2 * SSD_HEADS)))
    dsk = jnp.repeat(d_skip.astype(F32), SSD_HEAD_DIM).reshape(1, d_inner)
    times = _chunk_times()
    prefix = (times[None, :] <= times[:, None]).astype(np.float32)
    st_shape = pltpu.VMEM((SSD_GROUPS, SSD_D_STATE, d_inner // SSD_GROUPS), F32)
    fwd = lambda i: i
    bwd = lambda i: nb - 1 - i

    def specs(bidx, b_blk, c_blk):
        return [
            pl.BlockSpec((1, rows, d_inner), lambda bi, i: (bi, bidx(i), 0)),
            pl.BlockSpec((1, rows, bcw), lambda bi, i: (bi, bidx(i), b_blk)),
            pl.BlockSpec((1, rows, bcw), lambda bi, i: (bi, bidx(i), c_blk)),
            pl.BlockSpec((1, rows, LANES), lambda bi, i: (bi, bidx(i), 0)),
        ]

    y_shape = jax.ShapeDtypeStruct((bsz, seqlen, d_inner), F32)
    return pl.pallas_call(
        _ssd_scan_kernel,
        grid=(bsz, nb),
        in_specs=specs(fwd, 0, 1) + specs(bwd, 2, 3) + [_const_spec((1, LANES)), _const_spec((SSD_CHUNK, SSD_CHUNK)),
                                                        _const_spec((SSD_CHUNK, SSD_CHUNK)), _const_spec((1, d_inner))],
        out_specs=[pl.BlockSpec((1, rows, d_inner), lambda bi, i: (bi, fwd(i), 0)),
                   pl.BlockSpec((1, rows, d_inner), lambda bi, i: (bi, bwd(i), 0))],
        out_shape=[y_shape, y_shape],
        scratch_shapes=[st_shape, st_shape],
        compiler_params=_params("parallel", "arbitrary"),
        name="ssd_scan",
    )(xs, bc, bc, dt, xs, bc, bc, dt, a2, jnp.asarray(prefix, BF16), jnp.asarray(prefix.T, BF16), dsk)


def _ssd_out_kernel(h_ref, yf_ref, yb_ref, z_ref, g_ref, order_ref, w_ref, o_ref):
    q = SSD_CHUNK
    z = z_ref[...]
    y = _rms((yf_ref[...] + yb_ref[...]) * (z * _sigmoid(z)), g_ref[...]).astype(BF16)
    y_time = [_dot(order_ref[...], y[ci * q:(ci + 1) * q]).astype(BF16) for ci in range(y.shape[0] // q)]
    o_ref[...] = h_ref[...] + _dot(jnp.concatenate(y_time, axis=0), w_ref[...])


def _ssd_out(h, y_f, y_b, z, norm_g, w_out, tm=512):
    t, d = h.shape
    d_inner = z.shape[1]
    to_time_order = (_chunk_times()[None, :] == np.arange(SSD_CHUNK)[:, None]).astype(np.float32)
    row = lambda i: (i, 0)
    return pl.pallas_call(
        _ssd_out_kernel,
        grid=(t // tm,),
        in_specs=[pl.BlockSpec((tm, d), row)] + [pl.BlockSpec((tm, d_inner), row)] * 3
                 + [_const_spec((1, d_inner)), _const_spec((SSD_CHUNK, SSD_CHUNK)), _const_spec((d_inner, d))],
        out_specs=pl.BlockSpec((tm, d), row),
        out_shape=jax.ShapeDtypeStruct((t, d), F32),
        compiler_params=_params("parallel"),
        name="ssd_out",
    )(h, y_f, y_b, z, norm_g.reshape(1, d_inner), jnp.asarray(to_time_order, BF16), w_out.astype(BF16))


def _ssd_mixer(h, bsz, mix_g, w_in, conv_w, conv_b, dt_bias, a_log, d_skip, norm_g, w_out):
    t, d = h.shape
    seqlen = t // bsz
    d_inner = SSD_HEADS * SSD_HEAD_DIM
    bcw = SSD_GROUPS * SSD_D_STATE
    conv_dim = d_inner + 4 * bcw
    z, xs, bc, dt = _ssd_in(h, seqlen, mix_g, w_in, dt_bias, conv_w, conv_b, d_inner, conv_dim)
    y_f, y_b = _ssd_scan(xs.reshape(bsz, seqlen, d_inner), bc.reshape(bsz, seqlen, 4 * bcw),
                         dt.reshape(bsz, seqlen, LANES), a_log, d_skip)
    return _ssd_out(h, y_f.reshape(t, d_inner), y_b.reshape(t, d_inner), z, norm_g, w_out)


def _na_qkv_kernel(h_ref, g_ref, w_ref, avg_ref, qg_ref, kg_ref, q_ref, k_ref, v_ref, raw_ref, ms_ref):
    xn = _rms(h_ref[...], g_ref[...]).astype(BF16)
    d = q_ref.shape[1]
    gw = avg_ref.shape[0]
    n_groups = 2 * d // gw
    for c in range(n_groups):
        raw_ref[:, c * gw:(c + 1) * gw] = _dot(xn, w_ref[:, c * gw:(c + 1) * gw])
    v_ref[...] = _dot(xn, w_ref[:, 2 * d:3 * d]).astype(BF16)
    for c in range(n_groups):
        x = raw_ref[:, c * gw:(c + 1) * gw]
        hi, lo = _split2(x * x)
        ms_ref[:, c * gw:(c + 1) * gw] = _dot(hi, avg_ref[...]) + _dot(lo, avg_ref[...])
    for c in range(n_groups):
        cols = slice(c * gw, (c + 1) * gw)
        x = raw_ref[:, cols] * lax.rsqrt(ms_ref[:, cols] + RMS_EPS)
        if c < n_groups // 2:
            q_ref[:, cols] = (x * qg_ref[:, cols] * (NA_HEAD_DIM ** -0.5 * LOG2E)).astype(BF16)
        else:
            kcols = slice(c * gw - d, (c + 1) * gw - d)
            k_ref[:, kcols] = (x * kg_ref[:, kcols]).astype(BF16)


def _na_qkv(h, norm_g, w_qkv, q_norm, k_norm, tm=512, gw=256):
    t, d = h.shape
    n_heads = d // NA_HEAD_DIM
    same_head = np.arange(gw)[:, None] // NA_HEAD_DIM == np.arange(gw)[None, :] // NA_HEAD_DIM
    avg = jnp.asarray(same_head.astype(np.float32) / NA_HEAD_DIM, BF16)
    row = lambda i: (i, 0)
    out = jax.ShapeDtypeStruct((t, d), BF16)
    return pl.pallas_call(
        _na_qkv_kernel,
        grid=(t // tm,),
        in_specs=[pl.BlockSpec((tm, d), row), _const_spec((1, d)), _const_spec((d, 3 * d)), _const_spec((gw, gw)),
                  _const_spec((1, d)), _const_spec((1, d))],
        out_specs=[pl.BlockSpec((tm, d), row)] * 3,
        out_shape=[out, out, out],
        scratch_shapes=[pltpu.VMEM((tm, 2 * d), F32), pltpu.VMEM((tm, 2 * d), F32)],
        compiler_params=_params("parallel"),
        name="na_qkv",
    )(h, norm_g.reshape(1, d), w_qkv.astype(BF16), avg,
      jnp.tile(q_norm, n_heads).reshape(1, d), jnp.tile(k_norm, n_heads).reshape(1, d))


def _na_attn_kernel(q_ref, k_ref, v_ref, bias_ref, o_ref, s_ref, p_ref, *, rows):
    nk = NA_WIN_ROWS * GRID_W
    n_pairs = q_ref.shape[2] // LANES
    low = lax.broadcasted_iota(jnp.int32, (GRID_W, LANES), 1) < NA_HEAD_DIM
    starts, first_rel = [], []
    for i in range(NA_ROWS_PER_STEP):
        r = pl.program_id(1) * NA_ROWS_PER_STEP + i
        r0 = jnp.clip(r - NA_WIN_ROWS // 2, 0, rows - NA_WIN_ROWS)
        starts.append(pl.multiple_of(r0 * GRID_W, GRID_W))
        first_rel.append(r0 - r + NA_WIN_ROWS - 1)
    units = [(i, hp) for i in range(NA_ROWS_PER_STEP) for hp in range(n_pairs)]

    row_max = []
    for u, (i, hp) in enumerate(units):
        q2 = q_ref[0, i * GRID_W:(i + 1) * GRID_W, hp * LANES:(hp + 1) * LANES]
        zero = jnp.zeros_like(q2)
        qab = jnp.concatenate([jnp.where(low, q2, zero), jnp.where(low, zero, q2)], axis=0)
        k2 = k_ref[0, pl.ds(starts[i], nk), hp * LANES:(hp + 1) * LANES]
        s = lax.dot_general(qab, k2, (((1,), (1,)), ((), ())), preferred_element_type=F32)
        bias = jnp.concatenate(
            [jnp.concatenate([bias_ref[2 * hp, first_rel[i] + 2 * j], bias_ref[2 * hp + 1, first_rel[i] + 2 * j]],
                             axis=0) for j in range(NA_WIN_ROWS // 2)], axis=1)
        s = s + bias
        s_ref[u] = s
        row_max.append(jnp.max(s, axis=-1, keepdims=True))
    inv_sum = []
    for u in range(len(units)):
        e = jnp.exp2(s_ref[u] - row_max[u])
        inv_sum.append(1.0 / jnp.sum(e, axis=-1, keepdims=True))
        p_ref[u] = e.astype(BF16)
    for u, (i, hp) in enumerate(units):
        v2 = v_ref[0, pl.ds(starts[i], nk), hp * LANES:(hp + 1) * LANES]
        o2 = _dot(p_ref[u], v2) * inv_sum[u]
        o_ref[0, i * GRID_W:(i + 1) * GRID_W, hp * LANES:(hp + 1) * LANES] = (
            jnp.where(low, o2[:GRID_W], o2[GRID_W:]).astype(o_ref.dtype))


def _na_bias_kernel(rpb_ref, place_ref, mask_ref, o_ref):
    n_rel_cols = 2 * NA_WIN_COLS - 1
    n_rel_rows = 2 * NA_WIN_ROWS - 1
    base = pl.program_id(0) * (n_rel_rows * n_rel_cols)
    for r in range(n_rel_rows - 1):
        acc = mask_ref[...]
        for c in range(n_rel_cols):
            acc = acc + rpb_ref[base + r * n_rel_cols + c] * place_ref[c]
            acc = acc + rpb_ref[base + (r + 1) * n_rel_cols + c] * place_ref[n_rel_cols + c]
        o_ref[0, r] = acc


def _na_bias_tiles(rpb):
    n_heads, n_rel_rows, n_rel_cols = rpb.shape
    qc = np.arange(GRID_W)[:, None]
    kc = np.arange(GRID_W)[None, :]
    win_c0 = np.clip(qc - NA_WIN_COLS // 2, 0, GRID_W - NA_WIN_COLS)
    valid = (kc >= win_c0) & (kc < win_c0 + NA_WIN_COLS)
    place = np.zeros((2 * n_rel_cols, GRID_W, LANES), np.float32)
    for c in range(n_rel_cols):
        hit = ((kc - qc + NA_WIN_COLS - 1 == c) & valid).astype(np.float32)
        place[c, :, :GRID_W] = hit
        place[n_rel_cols + c, :, GRID_W:] = hit
    mask = np.tile(np.where(valid, 0.0, MASK_VALUE).astype(np.float32), (1, 2))
    return pl.pallas_call(
        _na_bias_kernel,
        grid=(n_heads,),
        in_specs=[pl.BlockSpec(memory_space=pltpu.SMEM), _const_spec(place.shape), _const_spec(mask.shape)],
        out_specs=pl.BlockSpec((1, n_rel_rows - 1, GRID_W, LANES), lambda hd: (hd, 0, 0, 0)),
        out_shape=jax.ShapeDtypeStruct((n_heads, n_rel_rows - 1, GRID_W, LANES), F32),
        compiler_params=_params("parallel"),
        name="na_bias",
    )((rpb.astype(F32) * LOG2E).reshape(-1), jnp.asarray(place), jnp.asarray(mask))


def _na_attn(q, k, v, rpb):
    bsz, seqlen, d = q.shape
    rows = seqlen // GRID_W
    assert rows >= NA_WIN_ROWS and rows % NA_ROWS_PER_STEP == 0
    bias = _na_bias_tiles(rpb)
    n_units = NA_ROWS_PER_STEP * (d // LANES)
    blk = NA_ROWS_PER_STEP * GRID_W
    return pl.pallas_call(
        functools.partial(_na_attn_kernel, rows=rows),
        grid=(bsz, rows // NA_ROWS_PER_STEP),
        in_specs=[pl.BlockSpec((1, blk, d), lambda bi, r: (bi, r, 0)),
                  pl.BlockSpec((1, seqlen, d), lambda bi, r: (bi, 0, 0)),
                  pl.BlockSpec((1, seqlen, d), lambda bi, r: (bi, 0, 0)),
                  _const_spec(bias.shape)],
        out_specs=pl.BlockSpec((1, blk, d), lambda bi, r: (bi, r, 0)),
        out_shape=jax.ShapeDtypeStruct((bsz, seqlen, d), BF16),
        scratch_shapes=[pltpu.VMEM((n_units, 2 * GRID_W, NA_WIN_ROWS * GRID_W), F32),
                        pltpu.VMEM((n_units, 2 * GRID_W, NA_WIN_ROWS * GRID_W), BF16)],
        compiler_params=_params("parallel", "arbitrary"),
        name="na_attn",
    )(q, k, v, bias)


def _proj_residual_kernel(h_ref, x_ref, w_ref, o_ref):
    o_ref[...] = h_ref[...] + _dot(x_ref[...], w_ref[...])


def _proj_residual(h, x, w, tm=512):
    t, d = h.shape
    row = lambda i: (i, 0)
    return pl.pallas_call(
        _proj_residual_kernel,
        grid=(t // tm,),
        in_specs=[pl.BlockSpec((tm, d), row), pl.BlockSpec((tm, x.shape[1]), row), _const_spec(w.shape)],
        out_specs=pl.BlockSpec((tm, d), row),
        out_shape=jax.ShapeDtypeStruct((t, d), F32),
        compiler_params=_params("parallel"),
        name="proj_residual",
    )(h, x, w.astype(BF16))


def _na_mixer(h, bsz, mix_g, w_qkv, q_norm, k_norm, rpb, w_out):
    t, d = h.shape
    seqlen = t // bsz
    q, k, v = _na_qkv(h, mix_g, w_qkv, q_norm, k_norm)
    shp = (bsz, seqlen, d)
    o = _na_attn(q.reshape(shp), k.reshape(shp), v.reshape(shp), rpb)
    return _proj_residual(h, o.reshape(t, d), w_out)


def kernel(x, p, ffn1_norm, ffn1_w_gu, ffn1_w_down, mix_norm, ffn2_norm, ffn2_w_gu, ffn2_w_down, ple_norm, ple_w_gate, ple_w_proj, ple_post_norm, ssd_w_in, ssd_conv_w, ssd_conv_b, ssd_dt_bias, ssd_a_log, ssd_d, ssd_norm, ssd_w_out, na_w_qkv, na_q_norm, na_k_norm, na_rpb, na_w_out):
    bsz, seqlen, d = x.shape
    depth = p.shape[0]
    t = bsz * seqlen
    h = x.reshape(t, d)
    p_all = p.reshape(depth, t, p.shape[-1])
    for i in range(depth):
        h = _ffn(h, ffn1_norm[i], ffn1_w_gu[i], ffn1_w_down[i])
        j = i // 2
        if i % 2 == 0:
            h = _ssd_mixer(h, bsz, mix_norm[i], ssd_w_in[j], ssd_conv_w[j], ssd_conv_b[j], ssd_dt_bias[j],
                           ssd_a_log[j], ssd_d[j], ssd_norm[j], ssd_w_out[j])
        else:
            h = _na_mixer(h, bsz, mix_norm[i], na_w_qkv[j], na_q_norm[j], na_k_norm[j], na_rpb[j], na_w_out[j])
        h = _ffn(h, ffn2_norm[i], ffn2_w_gu[i], ffn2_w_down[i],
                 ple=(p_all, i, ple_norm[i], ple_w_gate[i], ple_w_proj[i], ple_post_norm[i]))
    return h.reshape(bsz, seqlen, d)
```

```python
import functools
import math

import jax
import jax.numpy as jnp
import numpy as np
from jax import lax
from jax.experimental import pallas as pl
from jax.experimental.pallas import tpu as pltpu

F32 = jnp.float32
BF16 = jnp.bfloat16

RMS_EPS = 1e-6
SSD_HEAD_DIM = 64
SSD_HEADS = 32
SSD_GROUPS = 8
SSD_D_STATE = 128
SSD_CONV = 5
SSD_CHUNK = 128
GRID_W = 64
NA_HEAD_DIM = 64
NA_WIN_ROWS = 8
NA_WIN_COLS = 16
NA_ROWS_PER_STEP = 2
FFN_CHUNK = 256
SUBLANES = 8
LANES = 128
CONV_HALO = SUBLANES
SCAN_CHUNKS_PER_STEP = 2
VMEM_LIMIT = 56 * 1024 * 1024
MASK_VALUE = -1e30
LOG2E = math.log2(math.e)


def _params(*sem):
    return pltpu.CompilerParams(dimension_semantics=sem, vmem_limit_bytes=VMEM_LIMIT)


def _const_spec(shape):
    nd = len(shape)
    return pl.BlockSpec(shape, lambda *_: (0,) * nd, pipeline_mode=pl.Buffered(1))


def _rms(x, g):
    ms = jnp.mean(x * x, axis=-1, keepdims=True)
    return x * lax.rsqrt(ms + RMS_EPS) * g


def _sigmoid(x):
    return 1.0 / (1.0 + jnp.exp(-x))


def _dot(a, b):
    return jnp.dot(a, b, preferred_element_type=F32)


def _split2(x):
    hi = x.astype(BF16)
    lo = (x - hi.astype(F32)).astype(BF16)
    return hi, lo


def _split3(x):
    hi = x.astype(BF16)
    r = x - hi.astype(F32)
    mid = r.astype(BF16)
    lo = (r - mid.astype(F32)).astype(BF16)
    return hi, mid, lo


def _chunk_times():
    p = np.arange(SSD_CHUNK)
    return (SSD_CHUNK // SUBLANES) * (p % SUBLANES) + p // SUBLANES


def _ffn_body(h_ref, g_ref, wgu_ref, wd_ref, a_ref):
    x = h_ref[...]
    xn = _rms(x, g_ref[...]).astype(BF16)
    d_ff = wd_ref.shape[0]
    for c in range(d_ff // FFN_CHUNK):
        gate = _dot(xn, wgu_ref[:, c * FFN_CHUNK:(c + 1) * FFN_CHUNK])
        up = _dot(xn, wgu_ref[:, d_ff + c * FFN_CHUNK:d_ff + (c + 1) * FFN_CHUNK])
        a_ref[:, c * FFN_CHUNK:(c + 1) * FFN_CHUNK] = (gate * _sigmoid(gate) * up).astype(BF16)
    return x + 0.5 * _dot(a_ref[...], wd_ref[...])


def _ffn_kernel(h_ref, g_ref, wgu_ref, wd_ref, o_ref, a_ref):
    o_ref[...] = _ffn_body(h_ref, g_ref, wgu_ref, wd_ref, a_ref)


def _ffn_ple_kernel(h_ref, g_ref, wgu_ref, wd_ref, p_ref, pg_ref, wgate_ref, wproj_ref, ppg_ref, o_ref, a_ref):
    h2 = _ffn_body(h_ref, g_ref, wgu_ref, wd_ref, a_ref)
    gate = _sigmoid(_dot(_rms(h2, pg_ref[...]).astype(BF16), wgate_ref[...]))
    emb = _dot(p_ref[...].astype(BF16), wproj_ref[...])
    o_ref[...] = h2 + gate * _rms(emb, ppg_ref[...])


def _ffn(h, norm_g, w_gu, w_down, ple=None, tm=512):
    t, d = h.shape
    wgu, wd = w_gu.astype(BF16), w_down.astype(BF16)
    d_ff = wd.shape[0]
    assert d_ff % FFN_CHUNK == 0
    row = lambda i: (i, 0)
    in_specs = [pl.BlockSpec((tm, d), row), _const_spec((1, d)), _const_spec(wgu.shape), _const_spec(wd.shape)]
    args = [h, norm_g.reshape(1, d), wgu, wd]
    kern = _ffn_kernel
    if ple is not None:
        p_all, layer, ple_g, w_gate, w_proj, post_g = ple
        dp = p_all.shape[-1]
        in_specs += [pl.BlockSpec((pl.Squeezed(), tm, dp), lambda i: (layer, i, 0)), _const_spec((1, d)),
                     _const_spec((d, d)), _const_spec((dp, d)), _const_spec((1, d))]
        args += [p_all, ple_g.reshape(1, d), w_gate.astype(BF16), w_proj.astype(BF16), post_g.reshape(1, d)]
        kern = _ffn_ple_kernel
    return pl.pallas_call(
        kern,
        grid=(t // tm,),
        in_specs=in_specs,
        out_specs=pl.BlockSpec((tm, d), row),
        out_shape=jax.ShapeDtypeStruct((t, d), F32),
        scratch_shapes=[pltpu.VMEM((tm, d_ff), BF16)],
        compiler_params=_params("parallel"),
        name="ffn_ple" if ple is not None else "ffn",
    )(*args)


def _ssd_in_kernel(h_ref, prev_ref, next_ref, g_ref, gather_ref, tail_ref, wz_ref, wx_ref, wdt_ref, dtb_ref,
                   cw_ref, cb_ref, z_ref, xs_ref, bc_ref, dt_ref, src_ref, proj_ref, *, tiles_per_seq):
    tm = h_ref.shape[0]
    q = SSD_CHUNK
    n_sub = tm // q
    pos = lax.rem(pl.program_id(0), tiles_per_seq)

    src_ref[0:CONV_HALO, :] = jnp.where(pos > 0, prev_ref[...], 0.0)
    src_ref[CONV_HALO:CONV_HALO + tm, :] = h_ref[...]
    src_ref[CONV_HALO + tm:, :] = jnp.where(pos < tiles_per_seq - 1, next_ref[...], 0.0)
    xn = _rms(src_ref[...], g_ref[...]).astype(BF16)
    conv_rows, plain_rows = [], []
    for ci in range(n_sub):
        picked = _dot(gather_ref[...], xn[ci * q:(ci + 1) * q + 2 * CONV_HALO]).astype(BF16)
        conv_rows.append(picked[0:q])
        plain_rows.append(picked[q:])
    tails = _dot(tail_ref[...], xn).astype(BF16)
    lhs_conv = jnp.concatenate(conv_rows + [tails], axis=0)
    lhs = jnp.concatenate(plain_rows, axis=0)

    n_chunks = wx_ref.shape[0]
    cw = wx_ref.shape[2]
    n_xs = xs_ref.shape[1] // cw
    n_wrap = SSD_CONV - 1
    z_every = n_chunks * cw // z_ref.shape[1]
    last_sublane = lax.broadcasted_iota(jnp.int32, (SUBLANES, LANES), 0) == SUBLANES - 1
    proj_ref[0] = _dot(lhs_conv, wx_ref[0])
    for c in range(n_chunks):
        if c + 1 < n_chunks:
            proj_ref[(c + 1) % 2] = _dot(lhs_conv, wx_ref[c + 1])
        if c % z_every == z_every - 1:
            zc = (c // z_every) * cw
            z_ref[:, zc:zc + cw] = _dot(lhs, wz_ref[:, zc:zc + cw]).astype(BF16)
        for ci in range(n_sub):
            for lt in range(cw // LANES):
                col = c * cw + lt * LANES
                cur = proj_ref[c % 2, ci * q:(ci + 1) * q, lt * LANES:(lt + 1) * LANES]
                tail = proj_ref[c % 2, tm + SUBLANES * ci:tm + SUBLANES * (ci + 1), lt * LANES:(lt + 1) * LANES]
                wrap = []
                for m in range(n_wrap):
                    up = pltpu.roll(cur[SUBLANES * m:SUBLANES * (m + 1)], SUBLANES - 1, axis=0)
                    end = pltpu.roll(tail, (SUBLANES - 1 - m) % SUBLANES, axis=0)
                    wrap.append(jnp.where(last_sublane, end, up))
                taps = jnp.concatenate([cur] + wrap, axis=0)
                acc = cb_ref[:, col:col + LANES] + taps[0:q] * cw_ref[0:1, col:col + LANES]
                for j in range(1, SSD_CONV):
                    acc = acc + taps[SUBLANES * j:SUBLANES * j + q] * cw_ref[j:j + 1, col:col + LANES]
                out = acc * _sigmoid(acc)
                if c < n_xs:
                    xs_ref[ci * q:(ci + 1) * q, col:col + LANES] = out.astype(BF16)
                else:
                    bc_ref[ci * q:(ci + 1) * q, col - n_xs * cw:col - n_xs * cw + LANES] = out.astype(BF16)
    raw = _dot(lhs, wdt_ref[...]) + dtb_ref[...]
    dt_ref[...] = jnp.maximum(raw, 0.0) + jnp.log1p(jnp.exp(-jnp.abs(raw)))


def _one_hot_rows(cols, width):
    mat = np.zeros((len(cols), width), np.float32)
    mat[np.arange(len(cols)), cols] = 1.0
    return jnp.asarray(mat, BF16)


def _ssd_in_gather_matrices(tm):
    q = SSD_CHUNK
    half = SSD_CONV // 2
    times = _chunk_times()
    chunk = [CONV_HALO + t - half for t in times] + [CONV_HALO + t for t in times]
    tails = [CONV_HALO + ci * q + q + m - half for ci in range(tm // q) for m in range(SUBLANES)]
    return _one_hot_rows(chunk, q + 2 * CONV_HALO), _one_hot_rows(tails, tm + 2 * CONV_HALO)


def _ssd_in(h, seqlen, norm_g, w_in, dt_bias, conv_w, conv_b, d_inner, conv_dim, tm=512, cw=512):
    t, d = h.shape
    n_dt = w_in.shape[1] - d_inner - conv_dim
    wz = w_in[:, :d_inner].astype(BF16)
    wx = w_in[:, d_inner:d_inner + conv_dim].reshape(d, conv_dim // cw, cw).transpose(1, 0, 2).astype(BF16)
    wdt = jnp.pad(w_in[:, d_inner + conv_dim:], ((0, 0), (0, LANES - n_dt))).astype(BF16)
    dtb = jnp.pad(dt_bias.reshape(1, n_dt), ((0, 0), (0, LANES - n_dt)))
    cwt = jnp.pad(conv_w, ((0, SUBLANES - SSD_CONV), (0, 0)))
    assert (tm // SSD_CHUNK) % 2 == 0
    gather, tail = _ssd_in_gather_matrices(tm)
    hb = tm // CONV_HALO
    n_hblk = t // CONV_HALO
    row = lambda i: (i, 0)
    return pl.pallas_call(
        functools.partial(_ssd_in_kernel, tiles_per_seq=seqlen // tm),
        grid=(t // tm,),
        in_specs=[pl.BlockSpec((tm, d), row),
                  pl.BlockSpec((CONV_HALO, d), lambda i: (jnp.maximum(i * hb - 1, 0), 0)),
                  pl.BlockSpec((CONV_HALO, d), lambda i: (jnp.minimum((i + 1) * hb, n_hblk - 1), 0)),
                  _const_spec((1, d)), _const_spec(gather.shape), _const_spec(tail.shape), _const_spec(wz.shape),
                  _const_spec(wx.shape),
                  _const_spec(wdt.shape), _const_spec((1, LANES)), _const_spec(cwt.shape),
                  _const_spec((1, conv_dim))],
        out_specs=[pl.BlockSpec((tm, d_inner), row), pl.BlockSpec((tm, d_inner), row),
                   pl.BlockSpec((tm, conv_dim - d_inner), row), pl.BlockSpec((tm, LANES), row)],
        out_shape=[jax.ShapeDtypeStruct((t, d_inner), BF16), jax.ShapeDtypeStruct((t, d_inner), BF16),
                   jax.ShapeDtypeStruct((t, conv_dim - d_inner), BF16), jax.ShapeDtypeStruct((t, LANES), F32)],
        scratch_shapes=[pltpu.VMEM((tm + 2 * CONV_HALO, d), F32),
                        pltpu.VMEM((2, tm + SUBLANES * (tm // SSD_CHUNK), cw), F32)],
        compiler_params=_params("parallel"),
        name="ssd_in",
    )(h, h, h, norm_g.reshape(1, d), gather, tail, wz, wx, wdt, dtb, cwt, conv_b.reshape(1, conv_dim))


def _scan_decays(dt_ref, rows, a2, tri, edge_row):
    dt = dt_ref[0, rows, :]
    hi, mid, lo = _split3(dt * a2)
    cum = _dot(tri, hi) + _dot(tri, mid) + _dot(tri, lo)
    dt_dec = dt * jnp.exp2(cum[edge_row:edge_row + 1, :] - cum)
    return cum, (cum - jnp.log2(dt)).T, dt_dec.T


def _scan_chunk(xs_ref, b_ref, c_ref, y_ref, st_ref, dsk_ref, rows, decays, masks, col0, edge_row):
    q = SSD_CHUNK
    hg = SSD_HEADS // SSD_GROUPS
    gw = hg * SSD_HEAD_DIM
    cum, key_t, dd_t = decays
    causal, lane_head, low_half = masks
    for g in range(SSD_GROUPS):
        bg = b_ref[0, rows, g * SSD_D_STATE:(g + 1) * SSD_D_STATE]
        cg = c_ref[0, rows, g * SSD_D_STATE:(g + 1) * SSD_D_STATE]
        cb = lax.dot_general(cg, bg, (((1,), (1,)), ((), ())), preferred_element_type=F32)
        bg_t = bg.astype(F32).T
        scores, b_scaled, ecum = [], [], []
        for j in range(hg):
            col = col0 + g * hg + j
            cum_q = jnp.broadcast_to(cum[:, col:col + 1], (q, q))
            weight = jnp.exp2(jnp.where(causal, cum_q - key_t[col:col + 1, :], -jnp.inf))
            scores.append((cb * weight).astype(BF16))
            b_scaled.append((bg_t * dd_t[col:col + 1, :]).astype(BF16))
            ecum.append(jnp.exp2(cum_q))
        xg = xs_ref[0, rows, g * gw:(g + 1) * gw]
        x_blocks = jnp.concatenate([jnp.where(lane_head == j, xg, jnp.zeros_like(xg)) for j in range(hg)],
                                   axis=0)
        y_diag = _dot(jnp.concatenate(scores, axis=1), x_blocks)
        ecum_g = jnp.concatenate([jnp.where(low_half, ecum[2 * i], ecum[2 * i + 1]) for i in range(hg // 2)],
                                 axis=1)
        st = st_ref[g]
        y = y_diag + _dot(cg, st.astype(BF16)) * ecum_g
        if dsk_ref is not None:
            y = y + xg.astype(F32) * dsk_ref[:, g * gw:(g + 1) * gw]
        y_ref[0, rows, g * gw:(g + 1) * gw] = y.astype(y_ref.dtype)
        new = _dot(jnp.concatenate(b_scaled, axis=1), x_blocks)
        st_ref[g] = st * ecum_g[edge_row:edge_row + 1, :] + new


def _ssd_scan_kernel(xsf_ref, bf_ref, cf_ref, dtf_ref, xsb_ref, bb_ref, cb_ref, dtb_ref, a_ref, trif_ref, trib_ref,
                     dsk_ref, yf_ref, yb_ref, stf_ref, stb_ref):
    q = SSD_CHUNK
    gw = (SSD_HEADS // SSD_GROUPS) * SSD_HEAD_DIM
    n_sub = xsf_ref.shape[1] // q

    @pl.when(pl.program_id(1) == 0)
    def _():
        stf_ref[...] = jnp.zeros_like(stf_ref)
        stb_ref[...] = jnp.zeros_like(stb_ref)

    def time_of(idx):
        return (q // SUBLANES) * (idx & (SUBLANES - 1)) + (idx >> 3)

    t_row = time_of(lax.broadcasted_iota(jnp.int32, (q, q), 0))
    t_col = time_of(lax.broadcasted_iota(jnp.int32, (q, q), 1))
    lane_head = lax.broadcasted_iota(jnp.int32, (q, gw), 1) // SSD_HEAD_DIM
    low_half = lax.broadcasted_iota(jnp.int32, (q, LANES), 1) < SSD_HEAD_DIM
    masks_f = (t_row >= t_col, lane_head, low_half)
    masks_b = (t_row <= t_col, lane_head, low_half)
    a2 = a_ref[...]
    order_f = [slice(ci * q, (ci + 1) * q) for ci in range(n_sub)]
    order_b = order_f[::-1]
    dec_f = [_scan_decays(dtf_ref, rows, a2, trif_ref[...], q - 1) for rows in order_f]
    dec_b = [_scan_decays(dtb_ref, rows, a2, trib_ref[...], 0) for rows in order_b]
    for i in range(n_sub):
        _scan_chunk(xsf_ref, bf_ref, cf_ref, yf_ref, stf_ref, dsk_ref, order_f[i], dec_f[i], masks_f, 0, q - 1)
        _scan_chunk(xsb_ref, bb_ref, cb_ref, yb_ref, stb_ref, None, order_b[i], dec_b[i], masks_b, SSD_HEADS, 0)


def _ssd_scan(xs, bc, dt, a_log, d_skip):
    bsz, seqlen, d_inner = xs.shape
    rows = SSD_CHUNK * min(SCAN_CHUNKS_PER_STEP, seqlen // SSD_CHUNK)
    nb = seqlen // rows
    bcw = SSD_GROUPS * SSD_D_STATE
    a2 = jnp.pad(-jnp.exp(a_log.astype(F32)).reshape(1, 2 * SSD_HEADS) * LOG2E,
                 ((0, 0), (0, LANES - 2 * SSD_HEADS)))
    dsk = jnp.repeat(d_skip.astype(F32), SSD_HEAD_DIM).reshape(1, d_inner)
    times = _chunk_times()
    prefix = (times[None, :] <= times[:, None]).astype(np.float32)
    st_shape = pltpu.VMEM((SSD_GROUPS, SSD_D_STATE, d_inner // SSD_GROUPS), F32)
    fwd = lambda i: i
    bwd = lambda i: nb - 1 - i

    def specs(bidx, b_blk, c_blk):
        return [
            pl.BlockSpec((1, rows, d_inner), lambda bi, i: (bi, bidx(i), 0)),
            pl.BlockSpec((1, rows, bcw), lambda bi, i: (bi, bidx(i), b_blk)),
            pl.BlockSpec((1, rows, bcw), lambda bi, i: (bi, bidx(i), c_blk)),
            pl.BlockSpec((1, rows, LANES), lambda bi, i: (bi, bidx(i), 0)),
        ]

    y_shape = jax.ShapeDtypeStruct((bsz, seqlen, d_inner), BF16)
    return pl.pallas_call(
        _ssd_scan_kernel,
        grid=(bsz, nb),
        in_specs=specs(fwd, 0, 1) + specs(bwd, 2, 3) + [_const_spec((1, LANES)), _const_spec((SSD_CHUNK, SSD_CHUNK)),
                                                        _const_spec((SSD_CHUNK, SSD_CHUNK)), _const_spec((1, d_inner))],
        out_specs=[pl.BlockSpec((1, rows, d_inner), lambda bi, i: (bi, fwd(i), 0)),
                   pl.BlockSpec((1, rows, d_inner), lambda bi, i: (bi, bwd(i), 0))],
        out_shape=[y_shape, y_shape],
        scratch_shapes=[st_shape, st_shape],
        compiler_params=_params("parallel", "arbitrary"),
        name="ssd_scan",
    )(xs, bc, bc, dt, xs, bc, bc, dt, a2, jnp.asarray(prefix, BF16), jnp.asarray(prefix.T, BF16), dsk)


def _ssd_out_kernel(h_ref, yf_ref, yb_ref, z_ref, g_ref, order_ref, w_ref, o_ref):
    q = SSD_CHUNK
    z = z_ref[...].astype(F32)
    y = yf_ref[...].astype(F32) + yb_ref[...].astype(F32)
    y = _rms(y * (z * _sigmoid(z)), g_ref[...]).astype(BF16)
    y_time = [_dot(order_ref[...], y[ci * q:(ci + 1) * q]).astype(BF16) for ci in range(y.shape[0] // q)]
    o_ref[...] = h_ref[...] + _dot(jnp.concatenate(y_time, axis=0), w_ref[...])


def _ssd_out(h, y_f, y_b, z, norm_g, w_out, tm=512):
    t, d = h.shape
    d_inner = z.shape[1]
    to_time_order = (_chunk_times()[None, :] == np.arange(SSD_CHUNK)[:, None]).astype(np.float32)
    row = lambda i: (i, 0)
    return pl.pallas_call(
        _ssd_out_kernel,
        grid=(t // tm,),
        in_specs=[pl.BlockSpec((tm, d), row)] + [pl.BlockSpec((tm, d_inner), row)] * 3
                 + [_const_spec((1, d_inner)), _const_spec((SSD_CHUNK, SSD_CHUNK)), _const_spec((d_inner, d))],
        out_specs=pl.BlockSpec((tm, d), row),
        out_shape=jax.ShapeDtypeStruct((t, d), F32),
        compiler_params=_params("parallel"),
        name="ssd_out",
    )(h, y_f, y_b, z, norm_g.reshape(1, d_inner), jnp.asarray(to_time_order, BF16), w_out.astype(BF16))


def _ssd_mixer(h, bsz, mix_g, w_in, conv_w, conv_b, dt_bias, a_log, d_skip, norm_g, w_out):
    t, d = h.shape
    seqlen = t // bsz
    d_inner = SSD_HEADS * SSD_HEAD_DIM
    bcw = SSD_GROUPS * SSD_D_STATE
    conv_dim = d_inner + 4 * bcw
    z, xs, bc, dt = _ssd_in(h, seqlen, mix_g, w_in, dt_bias, conv_w, conv_b, d_inner, conv_dim)
    y_f, y_b = _ssd_scan(xs.reshape(bsz, seqlen, d_inner), bc.reshape(bsz, seqlen, 4 * bcw),
                         dt.reshape(bsz, seqlen, LANES), a_log, d_skip)
    return _ssd_out(h, y_f.reshape(t, d_inner), y_b.reshape(t, d_inner), z, norm_g, w_out)


def _na_qkv_kernel(h_ref, g_ref, w_ref, avg_ref, qg_ref, kg_ref, q_ref, k_ref, v_ref, raw_ref, ms_ref):
    xn = _rms(h_ref[...], g_ref[...]).astype(BF16)
    d = q_ref.shape[1]
    gw = avg_ref.shape[0]
    n_groups = 2 * d // gw
    for c in range(n_groups):
        raw_ref[:, c * gw:(c + 1) * gw] = _dot(xn, w_ref[:, c * gw:(c + 1) * gw])
    v_ref[...] = _dot(xn, w_ref[:, 2 * d:3 * d]).astype(BF16)
    for c in range(n_groups):
        x = raw_ref[:, c * gw:(c + 1) * gw]
        hi, lo = _split2(x * x)
        ms_ref[:, c * gw:(c + 1) * gw] = _dot(hi, avg_ref[...]) + _dot(lo, avg_ref[...])
    for c in range(n_groups):
        cols = slice(c * gw, (c + 1) * gw)
        x = raw_ref[:, cols] * lax.rsqrt(ms_ref[:, cols] + RMS_EPS)
        if c < n_groups // 2:
            q_ref[:, cols] = (x * qg_ref[:, cols] * (NA_HEAD_DIM ** -0.5 * LOG2E)).astype(BF16)
        else:
            kcols = slice(c * gw - d, (c + 1) * gw - d)
            k_ref[:, kcols] = (x * kg_ref[:, kcols]).astype(BF16)


def _na_qkv(h, norm_g, w_qkv, q_norm, k_norm, tm=512, gw=256):
    t, d = h.shape
    n_heads = d // NA_HEAD_DIM
    same_head = np.arange(gw)[:, None] // NA_HEAD_DIM == np.arange(gw)[None, :] // NA_HEAD_DIM
    avg = jnp.asarray(same_head.astype(np.float32) / NA_HEAD_DIM, BF16)
    row = lambda i: (i, 0)
    out = jax.ShapeDtypeStruct((t, d), BF16)
    return pl.pallas_call(
        _na_qkv_kernel,
        grid=(t // tm,),
        in_specs=[pl.BlockSpec((tm, d), row), _const_spec((1, d)), _const_spec((d, 3 * d)), _const_spec((gw, gw)),
                  _const_spec((1, d)), _const_spec((1, d))],
        out_specs=[pl.BlockSpec((tm, d), row)] * 3,
        out_shape=[out, out, out],
        scratch_shapes=[pltpu.VMEM((tm, 2 * d), F32), pltpu.VMEM((tm, 2 * d), F32)],
        compiler_params=_params("parallel"),
        name="na_qkv",
    )(h, norm_g.reshape(1, d), w_qkv.astype(BF16), avg,
      jnp.tile(q_norm, n_heads).reshape(1, d), jnp.tile(k_norm, n_heads).reshape(1, d))


def _na_attn_kernel(q_ref, k_ref, v_ref, bias_ref, o_ref, s_ref, p_ref, *, rows):
    nk = NA_WIN_ROWS * GRID_W
    n_pairs = q_ref.shape[2] // LANES
    low = lax.broadcasted_iota(jnp.int32, (GRID_W, LANES), 1) < NA_HEAD_DIM
    starts, first_rel = [], []
    for i in range(NA_ROWS_PER_STEP):
        r = pl.program_id(1) * NA_ROWS_PER_STEP + i
        r0 = jnp.clip(r - NA_WIN_ROWS // 2, 0, rows - NA_WIN_ROWS)
        starts.append(pl.multiple_of(r0 * GRID_W, GRID_W))
        first_rel.append(r0 - r + NA_WIN_ROWS - 1)
    units = [(i, hp) for i in range(NA_ROWS_PER_STEP) for hp in range(n_pairs)]

    row_max = []
    for u, (i, hp) in enumerate(units):
        q2 = q_ref[0, i * GRID_W:(i + 1) * GRID_W, hp * LANES:(hp + 1) * LANES]
        zero = jnp.zeros_like(q2)
        qab = jnp.concatenate([jnp.where(low, q2, zero), jnp.where(low, zero, q2)], axis=0)
        k2 = k_ref[0, pl.ds(starts[i], nk), hp * LANES:(hp + 1) * LANES]
        s = lax.dot_general(qab, k2, (((1,), (1,)), ((), ())), preferred_element_type=F32)
        bias = jnp.concatenate(
            [jnp.concatenate([bias_ref[2 * hp, first_rel[i] + 2 * j], bias_ref[2 * hp + 1, first_rel[i] + 2 * j]],
                             axis=0) for j in range(NA_WIN_ROWS // 2)], axis=1)
        s = s + bias
        s_ref[u] = s
        row_max.append(jnp.max(s, axis=-1, keepdims=True))
    inv_sum = []
    for u in range(len(units)):
        e = jnp.exp2(s_ref[u] - row_max[u])
        inv_sum.append(1.0 / jnp.sum(e, axis=-1, keepdims=True))
        p_ref[u] = e.astype(BF16)
    for u, (i, hp) in enumerate(units):
        v2 = v_ref[0, pl.ds(starts[i], nk), hp * LANES:(hp + 1) * LANES]
        o2 = _dot(p_ref[u], v2) * inv_sum[u]
        o_ref[0, i * GRID_W:(i + 1) * GRID_W, hp * LANES:(hp + 1) * LANES] = (
            jnp.where(low, o2[:GRID_W], o2[GRID_W:]).astype(o_ref.dtype))


def _na_bias_kernel(rpb_ref, place_ref, mask_ref, o_ref):
    n_rel_cols = 2 * NA_WIN_COLS - 1
    n_rel_rows = 2 * NA_WIN_ROWS - 1
    base = pl.program_id(0) * (n_rel_rows * n_rel_cols)
    for r in range(n_rel_rows - 1):
        acc = mask_ref[...]
        for c in range(n_rel_cols):
            acc = acc + rpb_ref[base + r * n_rel_cols + c] * place_ref[c]
            acc = acc + rpb_ref[base + (r + 1) * n_rel_cols + c] * place_ref[n_rel_cols + c]
        o_ref[0, r] = acc


def _na_bias_tiles(rpb):
    n_heads, n_rel_rows, n_rel_cols = rpb.shape
    qc = np.arange(GRID_W)[:, None]
    kc = np.arange(GRID_W)[None, :]
    win_c0 = np.clip(qc - NA_WIN_COLS // 2, 0, GRID_W - NA_WIN_COLS)
    valid = (kc >= win_c0) & (kc < win_c0 + NA_WIN_COLS)
    place = np.zeros((2 * n_rel_cols, GRID_W, LANES), np.float32)
    for c in range(n_rel_cols):
        hit = ((kc - qc + NA_WIN_COLS - 1 == c) & valid).astype(np.float32)
        place[c, :, :GRID_W] = hit
        place[n_rel_cols + c, :, GRID_W:] = hit
    mask = np.tile(np.where(valid, 0.0, MASK_VALUE).astype(np.float32), (1, 2))
    return pl.pallas_call(
        _na_bias_kernel,
        grid=(n_heads,),
        in_specs=[pl.BlockSpec(memory_space=pltpu.SMEM), _const_spec(place.shape), _const_spec(mask.shape)],
        out_specs=pl.BlockSpec((1, n_rel_rows - 1, GRID_W, LANES), lambda hd: (hd, 0, 0, 0)),
        out_shape=jax.ShapeDtypeStruct((n_heads, n_rel_rows - 1, GRID_W, LANES), F32),
        compiler_params=_params("parallel"),
        name="na_bias",
    )((rpb.astype(F32) * LOG2E).reshape(-1), jnp.asarray(place), jnp.asarray(mask))


def _na_attn(q, k, v, rpb):
    bsz, seqlen, d = q.shape
    rows = seqlen // GRID_W
    assert rows >= NA_WIN_ROWS and rows % NA_ROWS_PER_STEP == 0
    bias = _na_bias_tiles(rpb)
    n_units = NA_ROWS_PER_STEP * (d // LANES)
    blk = NA_ROWS_PER_STEP * GRID_W
    return pl.pallas_call(
        functools.partial(_na_attn_kernel, rows=rows),
        grid=(bsz, rows // NA_ROWS_PER_STEP),
        in_specs=[pl.BlockSpec((1, blk, d), lambda bi, r: (bi, r, 0)),
                  pl.BlockSpec((1, seqlen, d), lambda bi, r: (bi, 0, 0)),
                  pl.BlockSpec((1, seqlen, d), lambda bi, r: (bi, 0, 0)),
                  _const_spec(bias.shape)],
        out_specs=pl.BlockSpec((1, blk, d), lambda bi, r: (bi, r, 0)),
        out_shape=jax.ShapeDtypeStruct((bsz, seqlen, d), BF16),
        scratch_shapes=[pltpu.VMEM((n_units, 2 * GRID_W, NA_WIN_ROWS * GRID_W), F32),
                        pltpu.VMEM((n_units, 2 * GRID_W, NA_WIN_ROWS * GRID_W), BF16)],
        compiler_params=_params("parallel", "arbitrary"),
        name="na_attn",
    )(q, k, v, bias)


def _proj_residual_kernel(h_ref, x_ref, w_ref, o_ref):
    o_ref[...] = h_ref[...] + _dot(x_ref[...], w_ref[...])


def _proj_residual(h, x, w, tm=512):
    t, d = h.shape
    row = lambda i: (i, 0)
    return pl.pallas_call(
        _proj_residual_kernel,
        grid=(t // tm,),
        in_specs=[pl.BlockSpec((tm, d), row), pl.BlockSpec((tm, x.shape[1]), row), _const_spec(w.shape)],
        out_specs=pl.BlockSpec((tm, d), row),
        out_shape=jax.ShapeDtypeStruct((t, d), F32),
        compiler_params=_params("parallel"),
        name="proj_residual",
    )(h, x, w.astype(BF16))


def _na_mixer(h, bsz, mix_g, w_qkv, q_norm, k_norm, rpb, w_out):
    t, d = h.shape
    seqlen = t // bsz
    q, k, v = _na_qkv(h, mix_g, w_qkv, q_norm, k_norm)
    shp = (bsz, seqlen, d)
    o = _na_attn(q.reshape(shp), k.reshape(shp), v.reshape(shp), rpb)
    return _proj_residual(h, o.reshape(t, d), w_out)


def kernel(x, p, ffn1_norm, ffn1_w_gu, ffn1_w_down, mix_norm, ffn2_norm, ffn2_w_gu, ffn2_w_down, ple_norm, ple_w_gate, ple_w_proj, ple_post_norm, ssd_w_in, ssd_conv_w, ssd_conv_b, ssd_dt_bias, ssd_a_log, ssd_d, ssd_norm, ssd_w_out, na_w_qkv, na_q_norm, na_k_norm, na_rpb, na_w_out):
    bsz, seqlen, d = x.shape
    depth = p.shape[0]
    t = bsz * seqlen
    h = x.reshape(t, d)
    p_all = p.reshape(depth, t, p.shape[-1])
    for i in range(depth):
        h = _ffn(h, ffn1_norm[i], ffn1_w_gu[i], ffn1_w_down[i])
        j = i // 2
        if i % 2 == 0:
            h = _ssd_mixer(h, bsz, mix_norm[i], ssd_w_in[j], ssd_conv_w[j], ssd_conv_b[j], ssd_dt_bias[j],
                           ssd_a_log[j], ssd_d[j], ssd_norm[j], ssd_w_out[j])
        else:
            h = _na_mixer(h, bsz, mix_norm[i], na_w_qkv[j], na_q_norm[j], na_k_norm[j], na_rpb[j], na_w_out[j])
        h = _ffn(h, ffn2_norm[i], ffn2_w_gu[i], ffn2_w_down[i],
                 ple=(p_all, i, ple_norm[i], ple_w_gate[i], ple_w_proj[i], ple_post_norm[i]))
    return h.reshape(bsz, seqlen, d)
```

```python
import functools
import math

import jax
import jax.numpy as jnp
import numpy as np
from jax import lax
from jax.experimental import pallas as pl
from jax.experimental.pallas import tpu as pltpu

F32 = jnp.float32
BF16 = jnp.bfloat16

RMS_EPS = 1e-6
SSD_HEAD_DIM = 64
SSD_HEADS = 32
SSD_GROUPS = 8
SSD_D_STATE = 128
SSD_CONV = 5
SSD_CHUNK = 128
GRID_W = 64
NA_HEAD_DIM = 64
NA_WIN_ROWS = 8
NA_WIN_COLS = 16
NA_ROWS_PER_STEP = 2
FFN_CHUNK = 256
SUBLANES = 8
LANES = 128
CONV_HALO = SUBLANES
SCAN_CHUNKS_PER_STEP = 4
VMEM_LIMIT = 56 * 1024 * 1024
MASK_VALUE = -1e30
LOG2E = math.log2(math.e)


def _params(*sem):
    return pltpu.CompilerParams(dimension_semantics=sem, vmem_limit_bytes=VMEM_LIMIT)


def _const_spec(shape):
    nd = len(shape)
    return pl.BlockSpec(shape, lambda *_: (0,) * nd, pipeline_mode=pl.Buffered(1))


def _rms(x, g):
    ms = jnp.mean(x * x, axis=-1, keepdims=True)
    return x * lax.rsqrt(ms + RMS_EPS) * g


def _sigmoid(x):
    return 1.0 / (1.0 + jnp.exp(-x))


def _dot(a, b):
    return jnp.dot(a, b, preferred_element_type=F32)


def _split2(x):
    hi = x.astype(BF16)
    lo = (x - hi.astype(F32)).astype(BF16)
    return hi, lo


def _split3(x):
    hi = x.astype(BF16)
    r = x - hi.astype(F32)
    mid = r.astype(BF16)
    lo = (r - mid.astype(F32)).astype(BF16)
    return hi, mid, lo


def _chunk_times():
    p = np.arange(SSD_CHUNK)
    return (SSD_CHUNK // SUBLANES) * (p % SUBLANES) + p // SUBLANES


def _ffn_kernel(*refs, has_pre, has_ple):
    refs = list(refs)
    h_ref = refs.pop(0)
    x = h_ref[...]
    if has_pre:
        pre_ref, wpre_ref = refs.pop(0), refs.pop(0)
        x = x + _dot(pre_ref[...], wpre_ref[...])
    g_ref, wgu_ref, wd_ref = refs.pop(0), refs.pop(0), refs.pop(0)
    o_ref, a_ref = refs[-2:]
    xn = _rms(x, g_ref[...]).astype(BF16)
    d_ff = wd_ref.shape[0]
    for c in range(d_ff // FFN_CHUNK):
        gate = _dot(xn, wgu_ref[:, c * FFN_CHUNK:(c + 1) * FFN_CHUNK])
        up = _dot(xn, wgu_ref[:, d_ff + c * FFN_CHUNK:d_ff + (c + 1) * FFN_CHUNK])
        a_ref[:, c * FFN_CHUNK:(c + 1) * FFN_CHUNK] = (gate * _sigmoid(gate) * up).astype(BF16)
    h2 = x + 0.5 * _dot(a_ref[...], wd_ref[...])
    if has_ple:
        p_ref, pg_ref, wgate_ref, wproj_ref, ppg_ref = refs[:5]
        gate = _sigmoid(_dot(_rms(h2, pg_ref[...]).astype(BF16), wgate_ref[...]))
        emb = _dot(p_ref[...].astype(BF16), wproj_ref[...])
        h2 = h2 + gate * _rms(emb, ppg_ref[...])
    o_ref[...] = h2


def _ffn(h, norm_g, w_gu, w_down, pre=None, ple=None, tm=512):
    t, d = h.shape
    wgu, wd = w_gu.astype(BF16), w_down.astype(BF16)
    d_ff = wd.shape[0]
    assert d_ff % FFN_CHUNK == 0
    row = lambda i: (i, 0)
    in_specs = [pl.BlockSpec((tm, d), row)]
    args = [h]
    if pre is not None:
        x_pre, w_pre = pre
        in_specs += [pl.BlockSpec((tm, x_pre.shape[1]), row), _const_spec(w_pre.shape)]
        args += [x_pre, w_pre.astype(BF16)]
    in_specs += [_const_spec((1, d)), _const_spec(wgu.shape), _const_spec(wd.shape)]
    args += [norm_g.reshape(1, d), wgu, wd]
    if ple is not None:
        p_all, layer, ple_g, w_gate, w_proj, post_g = ple
        dp = p_all.shape[-1]
        in_specs += [pl.BlockSpec((pl.Squeezed(), tm, dp), lambda i: (layer, i, 0)), _const_spec((1, d)),
                     _const_spec((d, d)), _const_spec((dp, d)), _const_spec((1, d))]
        args += [p_all, ple_g.reshape(1, d), w_gate.astype(BF16), w_proj.astype(BF16), post_g.reshape(1, d)]
    return pl.pallas_call(
        functools.partial(_ffn_kernel, has_pre=pre is not None, has_ple=ple is not None),
        grid=(t // tm,),
        in_specs=in_specs,
        out_specs=pl.BlockSpec((tm, d), row),
        out_shape=jax.ShapeDtypeStruct((t, d), F32),
        scratch_shapes=[pltpu.VMEM((tm, d_ff), BF16)],
        compiler_params=_params("parallel"),
        name="ffn" + ("_pre" if pre is not None else "") + ("_ple" if ple is not None else ""),
    )(*args)


def _ssd_in_kernel(h_ref, prev_ref, next_ref, g_ref, gather_ref, tail_ref, wz_ref, wx_ref, wdt_ref, dtb_ref,
                   cw_ref, cb_ref, z_ref, xs_ref, c_ref, bt_ref, dt_ref, src_ref, proj_ref, *, tiles_per_seq):
    tm = h_ref.shape[0]
    q = SSD_CHUNK
    n_sub = tm // q
    pos = lax.rem(pl.program_id(0), tiles_per_seq)

    src_ref[0:CONV_HALO, :] = jnp.where(pos > 0, prev_ref[...], 0.0)
    src_ref[CONV_HALO:CONV_HALO + tm, :] = h_ref[...]
    src_ref[CONV_HALO + tm:, :] = jnp.where(pos < tiles_per_seq - 1, next_ref[...], 0.0)
    xn = _rms(src_ref[...], g_ref[...]).astype(BF16)
    conv_rows, plain_rows = [], []
    for ci in range(n_sub):
        picked = _dot(gather_ref[...], xn[ci * q:(ci + 1) * q + 2 * CONV_HALO]).astype(BF16)
        conv_rows.append(picked[0:q])
        plain_rows.append(picked[q:])
    tails = _dot(tail_ref[...], xn).astype(BF16)
    lhs_conv = jnp.concatenate(conv_rows + [tails], axis=0)
    lhs = jnp.concatenate(plain_rows, axis=0)

    n_chunks = wx_ref.shape[0]
    cw = wx_ref.shape[2]
    n_xs = xs_ref.shape[1] // cw
    bcw = c_ref.shape[1] // 2
    n_wrap = SSD_CONV - 1
    z_every = n_chunks * cw // z_ref.shape[1]
    last_sublane = lax.broadcasted_iota(jnp.int32, (SUBLANES, LANES), 0) == SUBLANES - 1
    proj_ref[0] = _dot(lhs_conv, wx_ref[0])
    for c in range(n_chunks):
        if c + 1 < n_chunks:
            proj_ref[(c + 1) % 2] = _dot(lhs_conv, wx_ref[c + 1])
        if c % z_every == z_every - 1:
            zc = (c // z_every) * cw
            z_ref[:, zc:zc + cw] = _dot(lhs, wz_ref[:, zc:zc + cw]).astype(BF16)
        for ci in range(n_sub):
            for lt in range(cw // LANES):
                col = c * cw + lt * LANES
                cur = proj_ref[c % 2, ci * q:(ci + 1) * q, lt * LANES:(lt + 1) * LANES]
                tail = proj_ref[c % 2, tm + SUBLANES * ci:tm + SUBLANES * (ci + 1), lt * LANES:(lt + 1) * LANES]
                wrap = []
                for m in range(n_wrap):
                    up = pltpu.roll(cur[SUBLANES * m:SUBLANES * (m + 1)], SUBLANES - 1, axis=0)
                    end = pltpu.roll(tail, (SUBLANES - 1 - m) % SUBLANES, axis=0)
                    wrap.append(jnp.where(last_sublane, end, up))
                taps = jnp.concatenate([cur] + wrap, axis=0)
                acc = cb_ref[:, col:col + LANES] + taps[0:q] * cw_ref[0:1, col:col + LANES]
                for j in range(1, SSD_CONV):
                    acc = acc + taps[SUBLANES * j:SUBLANES * j + q] * cw_ref[j:j + 1, col:col + LANES]
                out = acc * _sigmoid(acc)
                if c < n_xs:
                    xs_ref[ci * q:(ci + 1) * q, col:col + LANES] = out.astype(BF16)
                else:
                    blk, off = divmod(col - n_xs * cw, bcw)
                    dst = (blk // 2) * bcw + off
                    if blk % 2 == 1:
                        c_ref[ci * q:(ci + 1) * q, dst:dst + LANES] = out.astype(BF16)
                    else:
                        bt_ref[ci, dst:dst + LANES, :] = out.T.astype(BF16)
    raw = _dot(lhs, wdt_ref[...]) + dtb_ref[...]
    dt_ref[...] = jnp.maximum(raw, 0.0) + jnp.log1p(jnp.exp(-jnp.abs(raw)))


def _one_hot_rows(cols, width):
    mat = np.zeros((len(cols), width), np.float32)
    mat[np.arange(len(cols)), cols] = 1.0
    return jnp.asarray(mat, BF16)


def _ssd_in_gather_matrices(tm):
    q = SSD_CHUNK
    half = SSD_CONV // 2
    times = _chunk_times()
    chunk = [CONV_HALO + t - half for t in times] + [CONV_HALO + t for t in times]
    tails = [CONV_HALO + ci * q + q + m - half for ci in range(tm // q) for m in range(SUBLANES)]
    return _one_hot_rows(chunk, q + 2 * CONV_HALO), _one_hot_rows(tails, tm + 2 * CONV_HALO)


def _ssd_in(h, seqlen, norm_g, w_in, dt_bias, conv_w, conv_b, d_inner, conv_dim, tm=512, cw=512):
    t, d = h.shape
    bc2 = (conv_dim - d_inner) // 2
    n_dt = w_in.shape[1] - d_inner - conv_dim
    wz = w_in[:, :d_inner].astype(BF16)
    wx = w_in[:, d_inner:d_inner + conv_dim].reshape(d, conv_dim // cw, cw).transpose(1, 0, 2).astype(BF16)
    wdt = jnp.pad(w_in[:, d_inner + conv_dim:], ((0, 0), (0, LANES - n_dt))).astype(BF16)
    dtb = jnp.pad(dt_bias.reshape(1, n_dt), ((0, 0), (0, LANES - n_dt)))
    cwt = jnp.pad(conv_w, ((0, SUBLANES - SSD_CONV), (0, 0)))
    assert (tm // SSD_CHUNK) % 2 == 0
    gather, tail = _ssd_in_gather_matrices(tm)
    hb = tm // CONV_HALO
    n_hblk = t // CONV_HALO
    row = lambda i: (i, 0)
    return pl.pallas_call(
        functools.partial(_ssd_in_kernel, tiles_per_seq=seqlen // tm),
        grid=(t // tm,),
        in_specs=[pl.BlockSpec((tm, d), row),
                  pl.BlockSpec((CONV_HALO, d), lambda i: (jnp.maximum(i * hb - 1, 0), 0)),
                  pl.BlockSpec((CONV_HALO, d), lambda i: (jnp.minimum((i + 1) * hb, n_hblk - 1), 0)),
                  _const_spec((1, d)), _const_spec(gather.shape), _const_spec(tail.shape), _const_spec(wz.shape),
                  _const_spec(wx.shape),
                  _const_spec(wdt.shape), _const_spec((1, LANES)), _const_spec(cwt.shape),
                  _const_spec((1, conv_dim))],
        out_specs=[pl.BlockSpec((tm, d_inner), row), pl.BlockSpec((tm, d_inner), row), pl.BlockSpec((tm, bc2), row),
                   pl.BlockSpec((tm // SSD_CHUNK, bc2, SSD_CHUNK), lambda i: (i, 0, 0)),
                   pl.BlockSpec((tm, LANES), row)],
        out_shape=[jax.ShapeDtypeStruct((t, d_inner), BF16), jax.ShapeDtypeStruct((t, d_inner), BF16),
                   jax.ShapeDtypeStruct((t, bc2), BF16),
                   jax.ShapeDtypeStruct((t // SSD_CHUNK, bc2, SSD_CHUNK), BF16),
                   jax.ShapeDtypeStruct((t, LANES), F32)],
        scratch_shapes=[pltpu.VMEM((tm + 2 * CONV_HALO, d), F32),
                        pltpu.VMEM((2, tm + SUBLANES * (tm // SSD_CHUNK), cw), F32)],
        compiler_params=_params("parallel"),
        name="ssd_in",
    )(h, h, h, norm_g.reshape(1, d), gather, tail, wz, wx, wdt, dtb, cwt, conv_b.reshape(1, conv_dim))


def _scan_decays(dt_ref, rows, a2, tri, edge_row):
    dt = dt_ref[0, rows, :]
    hi, mid, lo = _split3(dt * a2)
    cum = _dot(tri, hi) + _dot(tri, mid) + _dot(tri, lo)
    dt_dec = dt * jnp.exp2(cum[edge_row:edge_row + 1, :] - cum)
    return cum, (cum - jnp.log2(dt)).T, dt_dec.T


def _scan_chunk(xs_ref, bt_ref, c_ref, y_ref, st_ref, dsk_ref, ci, decays, masks, col0, edge_row):
    q = SSD_CHUNK
    hg = SSD_HEADS // SSD_GROUPS
    gw = hg * SSD_HEAD_DIM
    rows = slice(ci * q, (ci + 1) * q)
    cum, key_t, dd_t = decays
    causal, lane_head, low_half = masks
    for g in range(SSD_GROUPS):
        bg_t = bt_ref[0, ci, g * SSD_D_STATE:(g + 1) * SSD_D_STATE, :]
        cg = c_ref[0, rows, g * SSD_D_STATE:(g + 1) * SSD_D_STATE]
        cb = _dot(cg, bg_t)
        bg_t = bg_t.astype(F32)
        scores, b_scaled, ecum = [], [], []
        for j in range(hg):
            col = col0 + g * hg + j
            cum_q = jnp.broadcast_to(cum[:, col:col + 1], (q, q))
            weight = jnp.exp2(jnp.where(causal, cum_q - key_t[col:col + 1, :], -jnp.inf))
            scores.append((cb * weight).astype(BF16))
            b_scaled.append((bg_t * dd_t[col:col + 1, :]).astype(BF16))
            ecum.append(jnp.exp2(cum_q))
        xg = xs_ref[0, rows, g * gw:(g + 1) * gw]
        x_blocks = jnp.concatenate([jnp.where(lane_head == j, xg, jnp.zeros_like(xg)) for j in range(hg)],
                                   axis=0)
        y_diag = _dot(jnp.concatenate(scores, axis=1), x_blocks)
        ecum_g = jnp.concatenate([jnp.where(low_half, ecum[2 * i], ecum[2 * i + 1]) for i in range(hg // 2)],
                                 axis=1)
        st = st_ref[g]
        y = y_diag + _dot(cg, st.astype(BF16)) * ecum_g
        if dsk_ref is not None:
            y = y + xg.astype(F32) * dsk_ref[:, g * gw:(g + 1) * gw]
        y_ref[0, rows, g * gw:(g + 1) * gw] = y.astype(y_ref.dtype)
        new = _dot(jnp.concatenate(b_scaled, axis=1), x_blocks)
        st_ref[g] = st * ecum_g[edge_row:edge_row + 1, :] + new


def _ssd_scan_kernel(xsf_ref, bf_ref, cf_ref, dtf_ref, xsb_ref, bb_ref, cb_ref, dtb_ref, a_ref, trif_ref, trib_ref,
                     dsk_ref, yf_ref, yb_ref, stf_ref, stb_ref):
    q = SSD_CHUNK
    gw = (SSD_HEADS // SSD_GROUPS) * SSD_HEAD_DIM
    n_sub = xsf_ref.shape[1] // q

    @pl.when(pl.program_id(1) == 0)
    def _():
        stf_ref[...] = jnp.zeros_like(stf_ref)
        stb_ref[...] = jnp.zeros_like(stb_ref)

    def time_of(idx):
        return (q // SUBLANES) * (idx & (SUBLANES - 1)) + (idx >> 3)

    t_row = time_of(lax.broadcasted_iota(jnp.int32, (q, q), 0))
    t_col = time_of(lax.broadcasted_iota(jnp.int32, (q, q), 1))
    lane_head = lax.broadcasted_iota(jnp.int32, (q, gw), 1) // SSD_HEAD_DIM
    low_half = lax.broadcasted_iota(jnp.int32, (q, LANES), 1) < SSD_HEAD_DIM
    masks_f = (t_row >= t_col, lane_head, low_half)
    masks_b = (t_row <= t_col, lane_head, low_half)
    a2 = a_ref[...]
    order_f = list(range(n_sub))
    order_b = order_f[::-1]
    rows_of = lambda ci: slice(ci * q, (ci + 1) * q)
    dec_f = [_scan_decays(dtf_ref, rows_of(ci), a2, trif_ref[...], q - 1) for ci in order_f]
    dec_b = [_scan_decays(dtb_ref, rows_of(ci), a2, trib_ref[...], 0) for ci in order_b]
    for i in range(n_sub):
        _scan_chunk(xsf_ref, bf_ref, cf_ref, yf_ref, stf_ref, dsk_ref, order_f[i], dec_f[i], masks_f, 0, q - 1)
        _scan_chunk(xsb_ref, bb_ref, cb_ref, yb_ref, stb_ref, None, order_b[i], dec_b[i], masks_b, SSD_HEADS, 0)


def _ssd_scan(xs, c, bt, dt, a_log, d_skip):
    bsz, seqlen, d_inner = xs.shape
    n_sub = min(SCAN_CHUNKS_PER_STEP, seqlen // SSD_CHUNK)
    rows = SSD_CHUNK * n_sub
    nb = seqlen // rows
    bcw = SSD_GROUPS * SSD_D_STATE
    a2 = jnp.pad(-jnp.exp(a_log.astype(F32)).reshape(1, 2 * SSD_HEADS) * LOG2E,
                 ((0, 0), (0, LANES - 2 * SSD_HEADS)))
    dsk = jnp.repeat(d_skip.astype(F32), SSD_HEAD_DIM).reshape(1, d_inner)
    times = _chunk_times()
    prefix = (times[None, :] <= times[:, None]).astype(np.float32)
    st_shape = pltpu.VMEM((SSD_GROUPS, SSD_D_STATE, d_inner // SSD_GROUPS), F32)
    fwd = lambda i: i
    bwd = lambda i: nb - 1 - i

    def specs(bidx, direction):
        return [
            pl.BlockSpec((1, rows, d_inner), lambda bi, i: (bi, bidx(i), 0)),
            pl.BlockSpec((1, n_sub, bcw, SSD_CHUNK), lambda bi, i: (bi, bidx(i), direction, 0)),
            pl.BlockSpec((1, rows, bcw), lambda bi, i: (bi, bidx(i), direction)),
            pl.BlockSpec((1, rows, LANES), lambda bi, i: (bi, bidx(i), 0)),
        ]

    y_shape = jax.ShapeDtypeStruct((bsz, seqlen, d_inner), BF16)
    return pl.pallas_call(
        _ssd_scan_kernel,
        grid=(bsz, nb),
        in_specs=specs(fwd, 0) + specs(bwd, 1) + [_const_spec((1, LANES)), _const_spec((SSD_CHUNK, SSD_CHUNK)),
                                                  _const_spec((SSD_CHUNK, SSD_CHUNK)), _const_spec((1, d_inner))],
        out_specs=[pl.BlockSpec((1, rows, d_inner), lambda bi, i: (bi, fwd(i), 0)),
                   pl.BlockSpec((1, rows, d_inner), lambda bi, i: (bi, bwd(i), 0))],
        out_shape=[y_shape, y_shape],
        scratch_shapes=[st_shape, st_shape],
        compiler_params=_params("parallel", "arbitrary"),
        name="ssd_scan",
    )(xs, bt, c, dt, xs, bt, c, dt, a2, jnp.asarray(prefix, BF16), jnp.asarray(prefix.T, BF16), dsk)


def _ssd_out_kernel(h_ref, yf_ref, yb_ref, z_ref, g_ref, order_ref, w_ref, o_ref):
    q = SSD_CHUNK
    z = z_ref[...].astype(F32)
    y = yf_ref[...].astype(F32) + yb_ref[...].astype(F32)
    y = _rms(y * (z * _sigmoid(z)), g_ref[...]).astype(BF16)
    y_time = [_dot(order_ref[...], y[ci * q:(ci + 1) * q]).astype(BF16) for ci in range(y.shape[0] // q)]
    o_ref[...] = h_ref[...] + _dot(jnp.concatenate(y_time, axis=0), w_ref[...])


def _ssd_out(h, y_f, y_b, z, norm_g, w_out, tm=512):
    t, d = h.shape
    d_inner = z.shape[1]
    to_time_order = (_chunk_times()[None, :] == np.arange(SSD_CHUNK)[:, None]).astype(np.float32)
    row = lambda i: (i, 0)
    return pl.pallas_call(
        _ssd_out_kernel,
        grid=(t // tm,),
        in_specs=[pl.BlockSpec((tm, d), row)] + [pl.BlockSpec((tm, d_inner), row)] * 3
                 + [_const_spec((1, d_inner)), _const_spec((SSD_CHUNK, SSD_CHUNK)), _const_spec((d_inner, d))],
        out_specs=pl.BlockSpec((tm, d), row),
        out_shape=jax.ShapeDtypeStruct((t, d), F32),
        compiler_params=_params("parallel"),
        name="ssd_out",
    )(h, y_f, y_b, z, norm_g.reshape(1, d_inner), jnp.asarray(to_time_order, BF16), w_out.astype(BF16))


def _ssd_mixer(h, bsz, mix_g, w_in, conv_w, conv_b, dt_bias, a_log, d_skip, norm_g, w_out):
    t, d = h.shape
    seqlen = t // bsz
    d_inner = SSD_HEADS * SSD_HEAD_DIM
    bcw = SSD_GROUPS * SSD_D_STATE
    conv_dim = d_inner + 4 * bcw
    z, xs, c, bt, dt = _ssd_in(h, seqlen, mix_g, w_in, dt_bias, conv_w, conv_b, d_inner, conv_dim)
    y_f, y_b = _ssd_scan(xs.reshape(bsz, seqlen, d_inner), c.reshape(bsz, seqlen, 2 * bcw),
                         bt.reshape(bsz, seqlen // SSD_CHUNK, 2 * bcw, SSD_CHUNK),
                         dt.reshape(bsz, seqlen, LANES), a_log, d_skip)
    return _ssd_out(h, y_f.reshape(t, d_inner), y_b.reshape(t, d_inner), z, norm_g, w_out)


def _na_qkv_kernel(h_ref, g_ref, w_ref, avg_ref, qg_ref, kg_ref, q_ref, k_ref, v_ref, raw_ref, ms_ref):
    xn = _rms(h_ref[...], g_ref[...]).astype(BF16)
    d = q_ref.shape[1]
    gw = avg_ref.shape[0]
    n_groups = 2 * d // gw
    for c in range(n_groups):
        raw_ref[:, c * gw:(c + 1) * gw] = _dot(xn, w_ref[:, c * gw:(c + 1) * gw])
    v_ref[...] = _dot(xn, w_ref[:, 2 * d:3 * d]).astype(BF16)
    for c in range(n_groups):
        x = raw_ref[:, c * gw:(c + 1) * gw]
        hi, lo = _split2(x * x)
        ms_ref[:, c * gw:(c + 1) * gw] = _dot(hi, avg_ref[...]) + _dot(lo, avg_ref[...])
    for c in range(n_groups):
        cols = slice(c * gw, (c + 1) * gw)
        x = raw_ref[:, cols] * lax.rsqrt(ms_ref[:, cols] + RMS_EPS)
        if c < n_groups // 2:
            q_ref[:, cols] = (x * qg_ref[:, cols] * (NA_HEAD_DIM ** -0.5 * LOG2E)).astype(BF16)
        else:
            kcols = slice(c * gw - d, (c + 1) * gw - d)
            k_ref[:, kcols] = (x * kg_ref[:, kcols]).astype(BF16)


def _na_qkv(h, norm_g, w_qkv, q_norm, k_norm, tm=512, gw=256):
    t, d = h.shape
    n_heads = d // NA_HEAD_DIM
    same_head = np.arange(gw)[:, None] // NA_HEAD_DIM == np.arange(gw)[None, :] // NA_HEAD_DIM
    avg = jnp.asarray(same_head.astype(np.float32) / NA_HEAD_DIM, BF16)
    row = lambda i: (i, 0)
    out = jax.ShapeDtypeStruct((t, d), BF16)
    return pl.pallas_call(
        _na_qkv_kernel,
        grid=(t // tm,),
        in_specs=[pl.BlockSpec((tm, d), row), _const_spec((1, d)), _const_spec((d, 3 * d)), _const_spec((gw, gw)),
                  _const_spec((1, d)), _const_spec((1, d))],
        out_specs=[pl.BlockSpec((tm, d), row)] * 3,
        out_shape=[out, out, out],
        scratch_shapes=[pltpu.VMEM((tm, 2 * d), F32), pltpu.VMEM((tm, 2 * d), F32)],
        compiler_params=_params("parallel"),
        name="na_qkv",
    )(h, norm_g.reshape(1, d), w_qkv.astype(BF16), avg,
      jnp.tile(q_norm, n_heads).reshape(1, d), jnp.tile(k_norm, n_heads).reshape(1, d))


def _na_attn_kernel(q_ref, k_ref, v_ref, bias_ref, o_ref, s_ref, p_ref, *, rows):
    nk = NA_WIN_ROWS * GRID_W
    n_pairs = q_ref.shape[2] // LANES
    low = lax.broadcasted_iota(jnp.int32, (GRID_W, LANES), 1) < NA_HEAD_DIM
    starts, first_rel = [], []
    for i in range(NA_ROWS_PER_STEP):
        r = pl.program_id(1) * NA_ROWS_PER_STEP + i
        r0 = jnp.clip(r - NA_WIN_ROWS // 2, 0, rows - NA_WIN_ROWS)
        starts.append(pl.multiple_of(r0 * GRID_W, GRID_W))
        first_rel.append(r0 - r + NA_WIN_ROWS - 1)
    units = [(i, hp) for i in range(NA_ROWS_PER_STEP) for hp in range(n_pairs)]

    row_max = []
    for u, (i, hp) in enumerate(units):
        q2 = q_ref[0, i * GRID_W:(i + 1) * GRID_W, hp * LANES:(hp + 1) * LANES]
        zero = jnp.zeros_like(q2)
        qab = jnp.concatenate([jnp.where(low, q2, zero), jnp.where(low, zero, q2)], axis=0)
        k2 = k_ref[0, pl.ds(starts[i], nk), hp * LANES:(hp + 1) * LANES]
        s = lax.dot_general(qab, k2, (((1,), (1,)), ((), ())), preferred_element_type=F32)
        bias = jnp.concatenate(
            [jnp.concatenate([bias_ref[2 * hp, first_rel[i] + 2 * j], bias_ref[2 * hp + 1, first_rel[i] + 2 * j]],
                             axis=0) for j in range(NA_WIN_ROWS // 2)], axis=1)
        s = s + bias
        s_ref[u] = s
        row_max.append(jnp.max(s, axis=-1, keepdims=True))
    inv_sum = []
    for u in range(len(units)):
        e = jnp.exp2(s_ref[u] - row_max[u])
        inv_sum.append(1.0 / jnp.sum(e, axis=-1, keepdims=True))
        p_ref[u] = e.astype(BF16)
    for u, (i, hp) in enumerate(units):
        v2 = v_ref[0, pl.ds(starts[i], nk), hp * LANES:(hp + 1) * LANES]
        o2 = _dot(p_ref[u], v2) * inv_sum[u]
        o_ref[0, i * GRID_W:(i + 1) * GRID_W, hp * LANES:(hp + 1) * LANES] = (
            jnp.where(low, o2[:GRID_W], o2[GRID_W:]).astype(o_ref.dtype))


def _na_bias_kernel(rpb_ref, place_ref, mask_ref, o_ref):
    n_rel_cols = 2 * NA_WIN_COLS - 1
    n_rel_rows = 2 * NA_WIN_ROWS - 1
    base = pl.program_id(0) * (n_rel_rows * n_rel_cols)
    for r in range(n_rel_rows - 1):
        acc = mask_ref[...]
        for c in range(n_rel_cols):
            acc = acc + rpb_ref[base + r * n_rel_cols + c] * place_ref[c]
            acc = acc + rpb_ref[base + (r + 1) * n_rel_cols + c] * place_ref[n_rel_cols + c]
        o_ref[0, r] = acc


def _na_bias_tiles(rpb):
    n_heads, n_rel_rows, n_rel_cols = rpb.shape
    qc = np.arange(GRID_W)[:, None]
    kc = np.arange(GRID_W)[None, :]
    win_c0 = np.clip(qc - NA_WIN_COLS // 2, 0, GRID_W - NA_WIN_COLS)
    valid = (kc >= win_c0) & (kc < win_c0 + NA_WIN_COLS)
    place = np.zeros((2 * n_rel_cols, GRID_W, LANES), np.float32)
    for c in range(n_rel_cols):
        hit = ((kc - qc + NA_WIN_COLS - 1 == c) & valid).astype(np.float32)
        place[c, :, :GRID_W] = hit
        place[n_rel_cols + c, :, GRID_W:] = hit
    mask = np.tile(np.where(valid, 0.0, MASK_VALUE).astype(np.float32), (1, 2))
    return pl.pallas_call(
        _na_bias_kernel,
        grid=(n_heads,),
        in_specs=[pl.BlockSpec(memory_space=pltpu.SMEM), _const_spec(place.shape), _const_spec(mask.shape)],
        out_specs=pl.BlockSpec((1, n_rel_rows - 1, GRID_W, LANES), lambda hd: (hd, 0, 0, 0)),
        out_shape=jax.ShapeDtypeStruct((n_heads, n_rel_rows - 1, GRID_W, LANES), F32),
        compiler_params=_params("parallel"),
        name="na_bias",
    )((rpb.astype(F32) * LOG2E).reshape(-1), jnp.asarray(place), jnp.asarray(mask))


def _na_attn(q, k, v, rpb):
    bsz, seqlen, d = q.shape
    rows = seqlen // GRID_W
    assert rows >= NA_WIN_ROWS and rows % NA_ROWS_PER_STEP == 0
    bias = _na_bias_tiles(rpb)
    n_units = NA_ROWS_PER_STEP * (d // LANES)
    blk = NA_ROWS_PER_STEP * GRID_W
    return pl.pallas_call(
        functools.partial(_na_attn_kernel, rows=rows),
        grid=(bsz, rows // NA_ROWS_PER_STEP),
        in_specs=[pl.BlockSpec((1, blk, d), lambda bi, r: (bi, r, 0)),
                  pl.BlockSpec((1, seqlen, d), lambda bi, r: (bi, 0, 0)),
                  pl.BlockSpec((1, seqlen, d), lambda bi, r: (bi, 0, 0)),
                  _const_spec(bias.shape)],
        out_specs=pl.BlockSpec((1, blk, d), lambda bi, r: (bi, r, 0)),
        out_shape=jax.ShapeDtypeStruct((bsz, seqlen, d), BF16),
        scratch_shapes=[pltpu.VMEM((n_units, 2 * GRID_W, NA_WIN_ROWS * GRID_W), F32),
                        pltpu.VMEM((n_units, 2 * GRID_W, NA_WIN_ROWS * GRID_W), BF16)],
        compiler_params=_params("parallel", "arbitrary"),
        name="na_attn",
    )(q, k, v, bias)


def _na_mixer(h, bsz, mix_g, w_qkv, q_norm, k_norm, rpb):
    t, d = h.shape
    seqlen = t // bsz
    q, k, v = _na_qkv(h, mix_g, w_qkv, q_norm, k_norm)
    shp = (bsz, seqlen, d)
    return _na_attn(q.reshape(shp), k.reshape(shp), v.reshape(shp), rpb).reshape(t, d)


def kernel(x, p, ffn1_norm, ffn1_w_gu, ffn1_w_down, mix_norm, ffn2_norm, ffn2_w_gu, ffn2_w_down, ple_norm, ple_w_gate, ple_w_proj, ple_post_norm, ssd_w_in, ssd_conv_w, ssd_conv_b, ssd_dt_bias, ssd_a_log, ssd_d, ssd_norm, ssd_w_out, na_w_qkv, na_q_norm, na_k_norm, na_rpb, na_w_out):
    bsz, seqlen, d = x.shape
    depth = p.shape[0]
    t = bsz * seqlen
    h = x.reshape(t, d)
    p_all = p.reshape(depth, t, p.shape[-1])
    for i in range(depth):
        h = _ffn(h, ffn1_norm[i], ffn1_w_gu[i], ffn1_w_down[i])
        j = i // 2
        pre = None
        if i % 2 == 0:
            h = _ssd_mixer(h, bsz, mix_norm[i], ssd_w_in[j], ssd_conv_w[j], ssd_conv_b[j], ssd_dt_bias[j],
                           ssd_a_log[j], ssd_d[j], ssd_norm[j], ssd_w_out[j])
        else:
            pre = (_na_mixer(h, bsz, mix_norm[i], na_w_qkv[j], na_q_norm[j], na_k_norm[j], na_rpb[j]), na_w_out[j])
        h = _ffn(h, ffn2_norm[i], ffn2_w_gu[i], ffn2_w_down[i], pre=pre,
                 ple=(p_all, i, ple_norm[i], ple_w_gate[i], ple_w_proj[i], ple_post_norm[i]))
    return h.reshape(bsz, seqlen, d)
```

```python
import functools
import math

import jax
import jax.numpy as jnp
import numpy as np
from jax import lax
from jax.experimental import pallas as pl
from jax.experimental.pallas import tpu as pltpu

F32 = jnp.float32
BF16 = jnp.bfloat16

RMS_EPS = 1e-6
SSD_HEAD_DIM = 64
SSD_HEADS = 32
SSD_GROUPS = 8
SSD_D_STATE = 128
SSD_CONV = 5
SSD_CHUNK = 128
GRID_W = 64
NA_HEAD_DIM = 64
NA_WIN_ROWS = 8
NA_WIN_COLS = 16
NA_ROWS_PER_STEP = 2
FFN_CHUNK = 256
SUBLANES = 8
LANES = 128
CONV_HALO = SUBLANES
SCAN_CHUNKS_PER_STEP = 4
VMEM_LIMIT = 56 * 1024 * 1024
MASK_VALUE = -1e30
LOG2E = math.log2(math.e)


def _params(*sem):
    return pltpu.CompilerParams(dimension_semantics=sem, vmem_limit_bytes=VMEM_LIMIT)


def _const_spec(shape):
    nd = len(shape)
    return pl.BlockSpec(shape, lambda *_: (0,) * nd, pipeline_mode=pl.Buffered(1))


def _rms(x, g):
    ms = jnp.mean(x * x, axis=-1, keepdims=True)
    return x * lax.rsqrt(ms + RMS_EPS) * g


def _sigmoid(x):
    return 1.0 / (1.0 + jnp.exp(-x))


def _dot(a, b):
    return jnp.dot(a, b, preferred_element_type=F32)


def _split2(x):
    hi = x.astype(BF16)
    lo = (x - hi.astype(F32)).astype(BF16)
    return hi, lo


def _split3(x):
    hi = x.astype(BF16)
    r = x - hi.astype(F32)
    mid = r.astype(BF16)
    lo = (r - mid.astype(F32)).astype(BF16)
    return hi, mid, lo


def _chunk_times():
    p = np.arange(SSD_CHUNK)
    return (SSD_CHUNK // SUBLANES) * (p % SUBLANES) + p // SUBLANES


def _ffn_kernel(*refs, has_pre, has_ple):
    refs = list(refs)
    h_ref = refs.pop(0)
    x = h_ref[...]
    if has_pre:
        pre_ref, wpre_ref = refs.pop(0), refs.pop(0)
        x = x + _dot(pre_ref[...], wpre_ref[...])
    g_ref, wgu_ref, wd_ref = refs.pop(0), refs.pop(0), refs.pop(0)
    o_ref, a_ref = refs[-2:]
    xn = _rms(x, g_ref[...]).astype(BF16)
    d_ff = wd_ref.shape[0]
    for c in range(d_ff // FFN_CHUNK):
        gate = _dot(xn, wgu_ref[:, c * FFN_CHUNK:(c + 1) * FFN_CHUNK].astype(BF16))
        up = _dot(xn, wgu_ref[:, d_ff + c * FFN_CHUNK:d_ff + (c + 1) * FFN_CHUNK].astype(BF16))
        a_ref[:, c * FFN_CHUNK:(c + 1) * FFN_CHUNK] = (gate * _sigmoid(gate) * up).astype(BF16)
    h2 = x + 0.5 * _dot(a_ref[...], wd_ref[...].astype(BF16))
    if has_ple:
        p_ref, pg_ref, wgate_ref, wproj_ref, ppg_ref = refs[:5]
        gate = _sigmoid(_dot(_rms(h2, pg_ref[...]).astype(BF16), wgate_ref[...]))
        emb = _dot(p_ref[...].astype(BF16), wproj_ref[...])
        h2 = h2 + gate * _rms(emb, ppg_ref[...])
    o_ref[...] = h2


def _ffn(h, norm_g, w_gu, w_down, pre=None, ple=None, tm=512):
    t, d = h.shape
    wgu, wd = (w_gu, w_down) if pre is None else (w_gu.astype(BF16), w_down.astype(BF16))
    d_ff = wd.shape[0]
    assert d_ff % FFN_CHUNK == 0
    row = lambda i: (i, 0)
    in_specs = [pl.BlockSpec((tm, d), row)]
    args = [h]
    if pre is not None:
        x_pre, w_pre = pre
        in_specs += [pl.BlockSpec((tm, x_pre.shape[1]), row), _const_spec(w_pre.shape)]
        args += [x_pre, w_pre.astype(BF16)]
    in_specs += [_const_spec((1, d)), _const_spec(wgu.shape), _const_spec(wd.shape)]
    args += [norm_g.reshape(1, d), wgu, wd]
    if ple is not None:
        p_all, layer, ple_g, w_gate, w_proj, post_g = ple
        dp = p_all.shape[-1]
        in_specs += [pl.BlockSpec((pl.Squeezed(), tm, dp), lambda i: (layer, i, 0)), _const_spec((1, d)),
                     _const_spec((d, d)), _const_spec((dp, d)), _const_spec((1, d))]
        args += [p_all, ple_g.reshape(1, d), w_gate.astype(BF16), w_proj.astype(BF16), post_g.reshape(1, d)]
    return pl.pallas_call(
        functools.partial(_ffn_kernel, has_pre=pre is not None, has_ple=ple is not None),
        grid=(t // tm,),
        in_specs=in_specs,
        out_specs=pl.BlockSpec((tm, d), row),
        out_shape=jax.ShapeDtypeStruct((t, d), F32),
        scratch_shapes=[pltpu.VMEM((tm, d_ff), BF16)],
        compiler_params=_params("parallel"),
        name="ffn" + ("_pre" if pre is not None else "") + ("_ple" if ple is not None else ""),
    )(*args)


def _ssd_in_kernel(h_ref, prev_ref, next_ref, g_ref, gather_ref, tail_ref, wz_ref, wx_ref, wdt_ref, dtb_ref,
                   cw_ref, cb_ref, z_ref, xs_ref, c_ref, bt_ref, dt_ref, src_ref, proj_ref, *, tiles_per_seq):
    tm = h_ref.shape[0]
    q = SSD_CHUNK
    n_sub = tm // q
    pos = lax.rem(pl.program_id(0), tiles_per_seq)

    src_ref[0:CONV_HALO, :] = jnp.where(pos > 0, prev_ref[...], 0.0)
    src_ref[CONV_HALO:CONV_HALO + tm, :] = h_ref[...]
    src_ref[CONV_HALO + tm:, :] = jnp.where(pos < tiles_per_seq - 1, next_ref[...], 0.0)
    xn = _rms(src_ref[...], g_ref[...]).astype(BF16)
    conv_rows, plain_rows = [], []
    for ci in range(n_sub):
        picked = _dot(gather_ref[...], xn[ci * q:(ci + 1) * q + 2 * CONV_HALO]).astype(BF16)
        conv_rows.append(picked[0:q])
        plain_rows.append(picked[q:])
    tails = _dot(tail_ref[...], xn).astype(BF16)
    lhs_conv = jnp.concatenate(conv_rows + [tails], axis=0)
    lhs = jnp.concatenate(plain_rows, axis=0)

    n_chunks = wx_ref.shape[0]
    cw = wx_ref.shape[2]
    n_xs = xs_ref.shape[1] // cw
    bcw = c_ref.shape[1] // 2
    n_wrap = SSD_CONV - 1
    z_every = n_chunks * cw // z_ref.shape[1]
    last_sublane = lax.broadcasted_iota(jnp.int32, (SUBLANES, LANES), 0) == SUBLANES - 1
    proj_ref[0] = _dot(lhs_conv, wx_ref[0])
    for c in range(n_chunks):
        if c + 1 < n_chunks:
            proj_ref[(c + 1) % 2] = _dot(lhs_conv, wx_ref[c + 1])
        if c % z_every == z_every - 1:
            zc = (c // z_every) * cw
            z_ref[:, zc:zc + cw] = _dot(lhs, wz_ref[:, zc:zc + cw]).astype(BF16)
        for ci in range(n_sub):
            for lt in range(cw // LANES):
                col = c * cw + lt * LANES
                cur = proj_ref[c % 2, ci * q:(ci + 1) * q, lt * LANES:(lt + 1) * LANES]
                tail = proj_ref[c % 2, tm + SUBLANES * ci:tm + SUBLANES * (ci + 1), lt * LANES:(lt + 1) * LANES]
                wrap = []
                for m in range(n_wrap):
                    up = pltpu.roll(cur[SUBLANES * m:SUBLANES * (m + 1)], SUBLANES - 1, axis=0)
                    end = pltpu.roll(tail, (SUBLANES - 1 - m) % SUBLANES, axis=0)
                    wrap.append(jnp.where(last_sublane, end, up))
                taps = jnp.concatenate([cur] + wrap, axis=0)
                acc = cb_ref[:, col:col + LANES] + taps[0:q] * cw_ref[0:1, col:col + LANES]
                for j in range(1, SSD_CONV):
                    acc = acc + taps[SUBLANES * j:SUBLANES * j + q] * cw_ref[j:j + 1, col:col + LANES]
                out = acc * _sigmoid(acc)
                if c < n_xs:
                    xs_ref[ci * q:(ci + 1) * q, col:col + LANES] = out.astype(BF16)
                else:
                    blk, off = divmod(col - n_xs * cw, bcw)
                    dst = (blk // 2) * bcw + off
                    if blk % 2 == 1:
                        c_ref[ci * q:(ci + 1) * q, dst:dst + LANES] = out.astype(BF16)
                    else:
                        bt_ref[ci, dst:dst + LANES, :] = out.T.astype(BF16)
    raw = _dot(lhs, wdt_ref[...]) + dtb_ref[...]
    dt_ref[...] = jnp.maximum(raw, 0.0) + jnp.log1p(jnp.exp(-jnp.abs(raw)))


def _one_hot_rows(cols, width):
    mat = np.zeros((len(cols), width), np.float32)
    mat[np.arange(len(cols)), cols] = 1.0
    return jnp.asarray(mat, BF16)


def _ssd_in_gather_matrices(tm):
    q = SSD_CHUNK
    half = SSD_CONV // 2
    times = _chunk_times()
    chunk = [CONV_HALO + t - half for t in times] + [CONV_HALO + t for t in times]
    tails = [CONV_HALO + ci * q + q + m - half for ci in range(tm // q) for m in range(SUBLANES)]
    return _one_hot_rows(chunk, q + 2 * CONV_HALO), _one_hot_rows(tails, tm + 2 * CONV_HALO)


def _ssd_in(h, seqlen, norm_g, w_in, dt_bias, conv_w, conv_b, d_inner, conv_dim, tm=512, cw=512):
    t, d = h.shape
    bc2 = (conv_dim - d_inner) // 2
    n_dt = w_in.shape[1] - d_inner - conv_dim
    wz = w_in[:, :d_inner].astype(BF16)
    wx = w_in[:, d_inner:d_inner + conv_dim].reshape(d, conv_dim // cw, cw).transpose(1, 0, 2).astype(BF16)
    wdt = jnp.pad(w_in[:, d_inner + conv_dim:], ((0, 0), (0, LANES - n_dt))).astype(BF16)
    dtb = jnp.pad(dt_bias.reshape(1, n_dt), ((0, 0), (0, LANES - n_dt)))
    cwt = jnp.pad(conv_w, ((0, SUBLANES - SSD_CONV), (0, 0)))
    assert (tm // SSD_CHUNK) % 2 == 0
    gather, tail = _ssd_in_gather_matrices(tm)
    hb = tm // CONV_HALO
    n_hblk = t // CONV_HALO
    row = lambda i: (i, 0)
    return pl.pallas_call(
        functools.partial(_ssd_in_kernel, tiles_per_seq=seqlen // tm),
        grid=(t // tm,),
        in_specs=[pl.BlockSpec((tm, d), row),
                  pl.BlockSpec((CONV_HALO, d), lambda i: (jnp.maximum(i * hb - 1, 0), 0)),
                  pl.BlockSpec((CONV_HALO, d), lambda i: (jnp.minimum((i + 1) * hb, n_hblk - 1), 0)),
                  _const_spec((1, d)), _const_spec(gather.shape), _const_spec(tail.shape), _const_spec(wz.shape),
                  _const_spec(wx.shape),
                  _const_spec(wdt.shape), _const_spec((1, LANES)), _const_spec(cwt.shape),
                  _const_spec((1, conv_dim))],
        out_specs=[pl.BlockSpec((tm, d_inner), row), pl.BlockSpec((tm, d_inner), row), pl.BlockSpec((tm, bc2), row),
                   pl.BlockSpec((tm // SSD_CHUNK, bc2, SSD_CHUNK), lambda i: (i, 0, 0)),
                   pl.BlockSpec((tm, LANES), row)],
        out_shape=[jax.ShapeDtypeStruct((t, d_inner), BF16), jax.ShapeDtypeStruct((t, d_inner), BF16),
                   jax.ShapeDtypeStruct((t, bc2), BF16),
                   jax.ShapeDtypeStruct((t // SSD_CHUNK, bc2, SSD_CHUNK), BF16),
                   jax.ShapeDtypeStruct((t, LANES), F32)],
        scratch_shapes=[pltpu.VMEM((tm + 2 * CONV_HALO, d), F32),
                        pltpu.VMEM((2, tm + SUBLANES * (tm // SSD_CHUNK), cw), F32)],
        compiler_params=_params("parallel"),
        name="ssd_in",
    )(h, h, h, norm_g.reshape(1, d), gather, tail, wz, wx, wdt, dtb, cwt, conv_b.reshape(1, conv_dim))


def _scan_decays(dt_ref, rows, a2, tri, edge_row):
    dt = dt_ref[0, rows, :]
    hi, mid, lo = _split3(dt * a2)
    cum = _dot(tri, hi) + _dot(tri, mid) + _dot(tri, lo)
    dt_dec = dt * jnp.exp2(cum[edge_row:edge_row + 1, :] - cum)
    return cum, (cum - jnp.log2(dt)).T, dt_dec.T


def _scan_chunk(xs_ref, bt_ref, c_ref, y_ref, st_ref, dsk_ref, ci, decays, masks, col0, edge_row):
    q = SSD_CHUNK
    hg = SSD_HEADS // SSD_GROUPS
    gw = hg * SSD_HEAD_DIM
    rows = slice(ci * q, (ci + 1) * q)
    cum, key_t, dd_t = decays
    causal, lane_head, low_half = masks
    for g in range(SSD_GROUPS):
        bg_t = bt_ref[0, ci, g * SSD_D_STATE:(g + 1) * SSD_D_STATE, :]
        cg = c_ref[0, rows, g * SSD_D_STATE:(g + 1) * SSD_D_STATE]
        cb = _dot(cg, bg_t)
        bg_t = bg_t.astype(F32)
        scores, b_scaled, ecum = [], [], []
        for j in range(hg):
            col = col0 + g * hg + j
            cum_q = jnp.broadcast_to(cum[:, col:col + 1], (q, q))
            weight = jnp.exp2(jnp.where(causal, cum_q - key_t[col:col + 1, :], -jnp.inf))
            scores.append((cb * weight).astype(BF16))
            b_scaled.append((bg_t * dd_t[col:col + 1, :]).astype(BF16))
            ecum.append(jnp.exp2(cum_q))
        xg = xs_ref[0, rows, g * gw:(g + 1) * gw]
        x_blocks = jnp.concatenate([jnp.where(lane_head == j, xg, jnp.zeros_like(xg)) for j in range(hg)],
                                   axis=0)
        y_diag = _dot(jnp.concatenate(scores, axis=1), x_blocks)
        ecum_g = jnp.concatenate([jnp.where(low_half, ecum[2 * i], ecum[2 * i + 1]) for i in range(hg // 2)],
                                 axis=1)
        st = st_ref[g]
        y = y_diag + _dot(cg, st.astype(BF16)) * ecum_g
        if dsk_ref is not None:
            y = y + xg.astype(F32) * dsk_ref[:, g * gw:(g + 1) * gw]
        y_ref[0, rows, g * gw:(g + 1) * gw] = y.astype(y_ref.dtype)
        new = _dot(jnp.concatenate(b_scaled, axis=1), x_blocks)
        st_ref[g] = st * ecum_g[edge_row:edge_row + 1, :] + new


def _ssd_scan_kernel(xsf_ref, bf_ref, cf_ref, dtf_ref, xsb_ref, bb_ref, cb_ref, dtb_ref, a_ref, trif_ref, trib_ref,
                     dsk_ref, yf_ref, yb_ref, stf_ref, stb_ref):
    q = SSD_CHUNK
    gw = (SSD_HEADS // SSD_GROUPS) * SSD_HEAD_DIM
    n_sub = xsf_ref.shape[1] // q

    @pl.when(pl.program_id(1) == 0)
    def _():
        stf_ref[...] = jnp.zeros_like(stf_ref)
        stb_ref[...] = jnp.zeros_like(stb_ref)

    def time_of(idx):
        return (q // SUBLANES) * (idx & (SUBLANES - 1)) + (idx >> 3)

    t_row = time_of(lax.broadcasted_iota(jnp.int32, (q, q), 0))
    t_col = time_of(lax.broadcasted_iota(jnp.int32, (q, q), 1))
    lane_head = lax.broadcasted_iota(jnp.int32, (q, gw), 1) // SSD_HEAD_DIM
    low_half = lax.broadcasted_iota(jnp.int32, (q, LANES), 1) < SSD_HEAD_DIM
    masks_f = (t_row >= t_col, lane_head, low_half)
    masks_b = (t_row <= t_col, lane_head, low_half)
    a2 = a_ref[...]
    order_f = list(range(n_sub))
    order_b = order_f[::-1]
    rows_of = lambda ci: slice(ci * q, (ci + 1) * q)
    dec_f = [_scan_decays(dtf_ref, rows_of(ci), a2, trif_ref[...], q - 1) for ci in order_f]
    dec_b = [_scan_decays(dtb_ref, rows_of(ci), a2, trib_ref[...], 0) for ci in order_b]
    for i in range(n_sub):
        _scan_chunk(xsf_ref, bf_ref, cf_ref, yf_ref, stf_ref, dsk_ref, order_f[i], dec_f[i], masks_f, 0, q - 1)
        _scan_chunk(xsb_ref, bb_ref, cb_ref, yb_ref, stb_ref, None, order_b[i], dec_b[i], masks_b, SSD_HEADS, 0)


def _ssd_scan(xs, c, bt, dt, a_log, d_skip):
    bsz, seqlen, d_inner = xs.shape
    n_sub = min(SCAN_CHUNKS_PER_STEP, seqlen // SSD_CHUNK)
    rows = SSD_CHUNK * n_sub
    nb = seqlen // rows
    bcw = SSD_GROUPS * SSD_D_STATE
    a2 = jnp.pad(-jnp.exp(a_log.astype(F32)).reshape(1, 2 * SSD_HEADS) * LOG2E,
                 ((0, 0), (0, LANES - 2 * SSD_HEADS)))
    dsk = jnp.repeat(d_skip.astype(F32), SSD_HEAD_DIM).reshape(1, d_inner)
    times = _chunk_times()
    prefix = (times[None, :] <= times[:, None]).astype(np.float32)
    st_shape = pltpu.VMEM((SSD_GROUPS, SSD_D_STATE, d_inner // SSD_GROUPS), F32)
    fwd = lambda i: i
    bwd = lambda i: nb - 1 - i

    def specs(bidx, direction):
        return [
            pl.BlockSpec((1, rows, d_inner), lambda bi, i: (bi, bidx(i), 0)),
            pl.BlockSpec((1, n_sub, bcw, SSD_CHUNK), lambda bi, i: (bi, bidx(i), direction, 0)),
            pl.BlockSpec((1, rows, bcw), lambda bi, i: (bi, bidx(i), direction)),
            pl.BlockSpec((1, rows, LANES), lambda bi, i: (bi, bidx(i), 0)),
        ]

    y_shape = jax.ShapeDtypeStruct((bsz, seqlen, d_inner), BF16)
    return pl.pallas_call(
        _ssd_scan_kernel,
        grid=(bsz, nb),
        in_specs=specs(fwd, 0) + specs(bwd, 1) + [_const_spec((1, LANES)), _const_spec((SSD_CHUNK, SSD_CHUNK)),
                                                  _const_spec((SSD_CHUNK, SSD_CHUNK)), _const_spec((1, d_inner))],
        out_specs=[pl.BlockSpec((1, rows, d_inner), lambda bi, i: (bi, fwd(i), 0)),
                   pl.BlockSpec((1, rows, d_inner), lambda bi, i: (bi, bwd(i), 0))],
        out_shape=[y_shape, y_shape],
        scratch_shapes=[st_shape, st_shape],
        compiler_params=_params("parallel", "arbitrary"),
        name="ssd_scan",
    )(xs, bt, c, dt, xs, bt, c, dt, a2, jnp.asarray(prefix, BF16), jnp.asarray(prefix.T, BF16), dsk)


def _ssd_out_kernel(h_ref, yf_ref, yb_ref, z_ref, g_ref, order_ref, w_ref, o_ref):
    q = SSD_CHUNK
    z = z_ref[...].astype(F32)
    y = yf_ref[...].astype(F32) + yb_ref[...].astype(F32)
    y = _rms(y * (z * _sigmoid(z)), g_ref[...]).astype(BF16)
    y_time = [_dot(order_ref[...], y[ci * q:(ci + 1) * q]).astype(BF16) for ci in range(y.shape[0] // q)]
    o_ref[...] = h_ref[...] + _dot(jnp.concatenate(y_time, axis=0), w_ref[...].astype(BF16))


def _ssd_out(h, y_f, y_b, z, norm_g, w_out, tm=512):
    t, d = h.shape
    d_inner = z.shape[1]
    to_time_order = (_chunk_times()[None, :] == np.arange(SSD_CHUNK)[:, None]).astype(np.float32)
    row = lambda i: (i, 0)
    return pl.pallas_call(
        _ssd_out_kernel,
        grid=(t // tm,),
        in_specs=[pl.BlockSpec((tm, d), row)] + [pl.BlockSpec((tm, d_inner), row)] * 3
                 + [_const_spec((1, d_inner)), _const_spec((SSD_CHUNK, SSD_CHUNK)), _const_spec((d_inner, d))],
        out_specs=pl.BlockSpec((tm, d), row),
        out_shape=jax.ShapeDtypeStruct((t, d), F32),
        compiler_params=_params("parallel"),
        name="ssd_out",
    )(h, y_f, y_b, z, norm_g.reshape(1, d_inner), jnp.asarray(to_time_order, BF16), w_out)


def _ssd_mixer(h, bsz, mix_g, w_in, conv_w, conv_b, dt_bias, a_log, d_skip, norm_g, w_out):
    t, d = h.shape
    seqlen = t // bsz
    d_inner = SSD_HEADS * SSD_HEAD_DIM
    bcw = SSD_GROUPS * SSD_D_STATE
    conv_dim = d_inner + 4 * bcw
    z, xs, c, bt, dt = _ssd_in(h, seqlen, mix_g, w_in, dt_bias, conv_w, conv_b, d_inner, conv_dim)
    y_f, y_b = _ssd_scan(xs.reshape(bsz, seqlen, d_inner), c.reshape(bsz, seqlen, 2 * bcw),
                         bt.reshape(bsz, seqlen // SSD_CHUNK, 2 * bcw, SSD_CHUNK),
                         dt.reshape(bsz, seqlen, LANES), a_log, d_skip)
    return _ssd_out(h, y_f.reshape(t, d_inner), y_b.reshape(t, d_inner), z, norm_g, w_out)


def _na_qkv_kernel(h_ref, g_ref, w_ref, avg_ref, qg_ref, kg_ref, q_ref, k_ref, v_ref, raw_ref, ms_ref):
    xn = _rms(h_ref[...], g_ref[...]).astype(BF16)
    d = q_ref.shape[1]
    gw = avg_ref.shape[0]
    n_groups = 2 * d // gw
    for c in range(n_groups):
        raw_ref[:, c * gw:(c + 1) * gw] = _dot(xn, w_ref[:, c * gw:(c + 1) * gw].astype(BF16))
    v_ref[...] = _dot(xn, w_ref[:, 2 * d:3 * d].astype(BF16)).astype(BF16)
    for c in range(n_groups):
        x = raw_ref[:, c * gw:(c + 1) * gw]
        hi, lo = _split2(x * x)
        ms_ref[:, c * gw:(c + 1) * gw] = _dot(hi, avg_ref[...]) + _dot(lo, avg_ref[...])
    for c in range(n_groups):
        cols = slice(c * gw, (c + 1) * gw)
        x = raw_ref[:, cols] * lax.rsqrt(ms_ref[:, cols] + RMS_EPS)
        if c < n_groups // 2:
            q_ref[:, cols] = (x * qg_ref[:, cols] * (NA_HEAD_DIM ** -0.5 * LOG2E)).astype(BF16)
        else:
            kcols = slice(c * gw - d, (c + 1) * gw - d)
            k_ref[:, kcols] = (x * kg_ref[:, kcols]).astype(BF16)


def _na_qkv(h, norm_g, w_qkv, q_norm, k_norm, tm=512, gw=256):
    t, d = h.shape
    n_heads = d // NA_HEAD_DIM
    same_head = np.arange(gw)[:, None] // NA_HEAD_DIM == np.arange(gw)[None, :] // NA_HEAD_DIM
    avg = jnp.asarray(same_head.astype(np.float32) / NA_HEAD_DIM, BF16)
    row = lambda i: (i, 0)
    out = jax.ShapeDtypeStruct((t, d), BF16)
    return pl.pallas_call(
        _na_qkv_kernel,
        grid=(t // tm,),
        in_specs=[pl.BlockSpec((tm, d), row), _const_spec((1, d)), _const_spec((d, 3 * d)), _const_spec((gw, gw)),
                  _const_spec((1, d)), _const_spec((1, d))],
        out_specs=[pl.BlockSpec((tm, d), row)] * 3,
        out_shape=[out, out, out],
        scratch_shapes=[pltpu.VMEM((tm, 2 * d), F32), pltpu.VMEM((tm, 2 * d), F32)],
        compiler_params=_params("parallel"),
        name="na_qkv",
    )(h, norm_g.reshape(1, d), w_qkv, avg,
      jnp.tile(q_norm, n_heads).reshape(1, d), jnp.tile(k_norm, n_heads).reshape(1, d))


def _na_attn_kernel(q_ref, k_ref, v_ref, bias_ref, o_ref, s_ref, p_ref, *, rows):
    nk = NA_WIN_ROWS * GRID_W
    n_pairs = q_ref.shape[2] // LANES
    low = lax.broadcasted_iota(jnp.int32, (GRID_W, LANES), 1) < NA_HEAD_DIM
    starts, first_rel = [], []
    for i in range(NA_ROWS_PER_STEP):
        r = pl.program_id(1) * NA_ROWS_PER_STEP + i
        r0 = jnp.clip(r - NA_WIN_ROWS // 2, 0, rows - NA_WIN_ROWS)
        starts.append(pl.multiple_of(r0 * GRID_W, GRID_W))
        first_rel.append(r0 - r + NA_WIN_ROWS - 1)
    units = [(i, hp) for i in range(NA_ROWS_PER_STEP) for hp in range(n_pairs)]

    row_max = []
    for u, (i, hp) in enumerate(units):
        q2 = q_ref[0, i * GRID_W:(i + 1) * GRID_W, hp * LANES:(hp + 1) * LANES]
        zero = jnp.zeros_like(q2)
        qab = jnp.concatenate([jnp.where(low, q2, zero), jnp.where(low, zero, q2)], axis=0)
        k2 = k_ref[0, pl.ds(starts[i], nk), hp * LANES:(hp + 1) * LANES]
        s = lax.dot_general(qab, k2, (((1,), (1,)), ((), ())), preferred_element_type=F32)
        bias = jnp.concatenate(
            [jnp.concatenate([bias_ref[2 * hp, first_rel[i] + 2 * j], bias_ref[2 * hp + 1, first_rel[i] + 2 * j]],
                             axis=0) for j in range(NA_WIN_ROWS // 2)], axis=1)
        s = s + bias
        s_ref[u] = s
        row_max.append(jnp.max(s, axis=-1, keepdims=True))
    inv_sum = []
    for u in range(len(units)):
        e = jnp.exp2(s_ref[u] - row_max[u])
        inv_sum.append(1.0 / jnp.sum(e, axis=-1, keepdims=True))
        p_ref[u] = e.astype(BF16)
    for u, (i, hp) in enumerate(units):
        v2 = v_ref[0, pl.ds(starts[i], nk), hp * LANES:(hp + 1) * LANES]
        o2 = _dot(p_ref[u], v2) * inv_sum[u]
        o_ref[0, i * GRID_W:(i + 1) * GRID_W, hp * LANES:(hp + 1) * LANES] = (
            jnp.where(low, o2[:GRID_W], o2[GRID_W:]).astype(o_ref.dtype))


def _na_bias_kernel(rpb_ref, place_ref, mask_ref, o_ref):
    n_rel_cols = 2 * NA_WIN_COLS - 1
    n_rel_rows = 2 * NA_WIN_ROWS - 1
    base = pl.program_id(0) * (n_rel_rows * n_rel_cols)
    for r in range(n_rel_rows - 1):
        acc = mask_ref[...]
        for c in range(n_rel_cols):
            acc = acc + rpb_ref[base + r * n_rel_cols + c] * place_ref[c]
            acc = acc + rpb_ref[base + (r + 1) * n_rel_cols + c] * place_ref[n_rel_cols + c]
        o_ref[0, r] = acc


def _na_bias_tiles(rpb):
    n_heads, n_rel_rows, n_rel_cols = rpb.shape
    qc = np.arange(GRID_W)[:, None]
    kc = np.arange(GRID_W)[None, :]
    win_c0 = np.clip(qc - NA_WIN_COLS // 2, 0, GRID_W - NA_WIN_COLS)
    valid = (kc >= win_c0) & (kc < win_c0 + NA_WIN_COLS)
    place = np.zeros((2 * n_rel_cols, GRID_W, LANES), np.float32)
    for c in range(n_rel_cols):
        hit = ((kc - qc + NA_WIN_COLS - 1 == c) & valid).astype(np.float32)
        place[c, :, :GRID_W] = hit
        place[n_rel_cols + c, :, GRID_W:] = hit
    mask = np.tile(np.where(valid, 0.0, MASK_VALUE).astype(np.float32), (1, 2))
    return pl.pallas_call(
        _na_bias_kernel,
        grid=(n_heads,),
        in_specs=[pl.BlockSpec(memory_space=pltpu.SMEM), _const_spec(place.shape), _const_spec(mask.shape)],
        out_specs=pl.BlockSpec((1, n_rel_rows - 1, GRID_W, LANES), lambda hd: (hd, 0, 0, 0)),
        out_shape=jax.ShapeDtypeStruct((n_heads, n_rel_rows - 1, GRID_W, LANES), F32),
        compiler_params=_params("parallel"),
        name="na_bias",
    )((rpb.astype(F32) * LOG2E).reshape(-1), jnp.asarray(place), jnp.asarray(mask))


def _na_attn(q, k, v, rpb):
    bsz, seqlen, d = q.shape
    rows = seqlen // GRID_W
    assert rows >= NA_WIN_ROWS and rows % NA_ROWS_PER_STEP == 0
    bias = _na_bias_tiles(rpb)
    n_units = NA_ROWS_PER_STEP * (d // LANES)
    blk = NA_ROWS_PER_STEP * GRID_W
    return pl.pallas_call(
        functools.partial(_na_attn_kernel, rows=rows),
        grid=(bsz, rows // NA_ROWS_PER_STEP),
        in_specs=[pl.BlockSpec((1, blk, d), lambda bi, r: (bi, r, 0)),
                  pl.BlockSpec((1, seqlen, d), lambda bi, r: (bi, 0, 0)),
                  pl.BlockSpec((1, seqlen, d), lambda bi, r: (bi, 0, 0)),
                  _const_spec(bias.shape)],
        out_specs=pl.BlockSpec((1, blk, d), lambda bi, r: (bi, r, 0)),
        out_shape=jax.ShapeDtypeStruct((bsz, seqlen, d), BF16),
        scratch_shapes=[pltpu.VMEM((n_units, 2 * GRID_W, NA_WIN_ROWS * GRID_W), F32),
                        pltpu.VMEM((n_units, 2 * GRID_W, NA_WIN_ROWS * GRID_W), BF16)],
        compiler_params=_params("parallel", "arbitrary"),
        name="na_attn",
    )(q, k, v, bias)


def _na_mixer(h, bsz, mix_g, w_qkv, q_norm, k_norm, rpb):
    t, d = h.shape
    seqlen = t // bsz
    q, k, v = _na_qkv(h, mix_g, w_qkv, q_norm, k_norm)
    shp = (bsz, seqlen, d)
    return _na_attn(q.reshape(shp), k.reshape(shp), v.reshape(shp), rpb).reshape(t, d)


def kernel(x, p, ffn1_norm, ffn1_w_gu, ffn1_w_down, mix_norm, ffn2_norm, ffn2_w_gu, ffn2_w_down, ple_norm, ple_w_gate, ple_w_proj, ple_post_norm, ssd_w_in, ssd_conv_w, ssd_conv_b, ssd_dt_bias, ssd_a_log, ssd_d, ssd_norm, ssd_w_out, na_w_qkv, na_q_norm, na_k_norm, na_rpb, na_w_out):
    bsz, seqlen, d = x.shape
    depth = p.shape[0]
    t = bsz * seqlen
    h = x.reshape(t, d)
    p_all = p.reshape(depth, t, p.shape[-1])
    for i in range(depth):
        h = _ffn(h, ffn1_norm[i], ffn1_w_gu[i], ffn1_w_down[i])
        j = i // 2
        pre = None
        if i % 2 == 0:
            h = _ssd_mixer(h, bsz, mix_norm[i], ssd_w_in[j], ssd_conv_w[j], ssd_conv_b[j], ssd_dt_bias[j],
                           ssd_a_log[j], ssd_d[j], ssd_norm[j], ssd_w_out[j])
        else:
            pre = (_na_mixer(h, bsz, mix_norm[i], na_w_qkv[j], na_q_norm[j], na_k_norm[j], na_rpb[j]), na_w_out[j])
        h = _ffn(h, ffn2_norm[i], ffn2_w_gu[i], ffn2_w_down[i], pre=pre,
                 ple=(p_all, i, ple_norm[i], ple_w_gate[i], ple_w_proj[i], ple_post_norm[i]))
    return h.reshape(bsz, seqlen, d)
```

```python
import functools
import math

import jax
import jax.numpy as jnp
import numpy as np
from jax import lax
from jax.experimental import pallas as pl
from jax.experimental.pallas import tpu as pltpu

F32 = jnp.float32
BF16 = jnp.bfloat16

RMS_EPS = 1e-6
SSD_HEAD_DIM = 64
SSD_HEADS = 32
SSD_GROUPS = 8
SSD_D_STATE = 128
SSD_CONV = 5
SSD_CHUNK = 128
GRID_W = 64
NA_HEAD_DIM = 64
NA_WIN_ROWS = 8
NA_WIN_COLS = 16
NA_ROWS_PER_STEP = 2
FFN_CHUNK = 256
SUBLANES = 8
LANES = 128
CONV_HALO = SUBLANES
SCAN_CHUNKS_PER_STEP = 4
VMEM_LIMIT = 56 * 1024 * 1024
MASK_VALUE = -1e30
LOG2E = math.log2(math.e)


def _params(*sem):
    return pltpu.CompilerParams(dimension_semantics=sem, vmem_limit_bytes=VMEM_LIMIT)


def _const_spec(shape):
    nd = len(shape)
    return pl.BlockSpec(shape, lambda *_: (0,) * nd, pipeline_mode=pl.Buffered(1))


def _layer_spec(stacked, layer):
    nd = stacked.ndim - 1
    return pl.BlockSpec((pl.Squeezed(),) + stacked.shape[1:], lambda *_: (layer,) + (0,) * nd,
                        pipeline_mode=pl.Buffered(1))


def _rms(x, g):
    ms = jnp.mean(x * x, axis=-1, keepdims=True)
    return x * lax.rsqrt(ms + RMS_EPS) * g


def _sigmoid(x):
    return 1.0 / (1.0 + jnp.exp(-x))


def _dot(a, b):
    return jnp.dot(a, b, preferred_element_type=F32)


def _split2(x):
    hi = x.astype(BF16)
    lo = (x - hi.astype(F32)).astype(BF16)
    return hi, lo


def _split3(x):
    hi = x.astype(BF16)
    r = x - hi.astype(F32)
    mid = r.astype(BF16)
    lo = (r - mid.astype(F32)).astype(BF16)
    return hi, mid, lo


def _chunk_times():
    p = np.arange(SSD_CHUNK)
    return (SSD_CHUNK // SUBLANES) * (p % SUBLANES) + p // SUBLANES


def _ffn_kernel(*refs, has_pre, has_ple):
    refs = list(refs)
    h_ref = refs.pop(0)
    x = h_ref[...]
    if has_pre:
        pre_ref, wpre_ref = refs.pop(0), refs.pop(0)
        x = x + _dot(pre_ref[...], wpre_ref[...])
    g_ref, wgu_ref, wd_ref = refs.pop(0), refs.pop(0), refs.pop(0)
    o_ref, a_ref = refs[-2:]
    xn = _rms(x, g_ref[...]).astype(BF16)
    d_ff = wd_ref.shape[0]
    for c in range(d_ff // FFN_CHUNK):
        gate = _dot(xn, wgu_ref[:, c * FFN_CHUNK:(c + 1) * FFN_CHUNK].astype(BF16))
        up = _dot(xn, wgu_ref[:, d_ff + c * FFN_CHUNK:d_ff + (c + 1) * FFN_CHUNK].astype(BF16))
        a_ref[:, c * FFN_CHUNK:(c + 1) * FFN_CHUNK] = (gate * _sigmoid(gate) * up).astype(BF16)
    h2 = x + 0.5 * _dot(a_ref[...], wd_ref[...].astype(BF16))
    if has_ple:
        p_ref, pg_ref, wgate_ref, wproj_ref, ppg_ref = refs[:5]
        gate = _sigmoid(_dot(_rms(h2, pg_ref[...]).astype(BF16), wgate_ref[...]))
        emb = _dot(p_ref[...].astype(BF16), wproj_ref[...])
        h2 = h2 + gate * _rms(emb, ppg_ref[...])
    o_ref[...] = h2


def _ffn(h, norm_g, w_gu_all, w_down_all, layer, pre=None, ple=None, tm=512):
    t, d = h.shape
    d_ff = w_down_all.shape[1]
    assert d_ff % FFN_CHUNK == 0
    row = lambda i: (i, 0)
    in_specs = [pl.BlockSpec((tm, d), row)]
    args = [h]
    if pre is None:
        w_specs = [_layer_spec(w_gu_all, layer), _layer_spec(w_down_all, layer)]
        w_args = [w_gu_all, w_down_all]
    else:
        x_pre, w_pre = pre
        in_specs += [pl.BlockSpec((tm, x_pre.shape[1]), row), _const_spec(w_pre.shape)]
        args += [x_pre, w_pre.astype(BF16)]
        w_args = [w_gu_all[layer].astype(BF16), w_down_all[layer].astype(BF16)]
        w_specs = [_const_spec(w.shape) for w in w_args]
    in_specs += [_const_spec((1, d))] + w_specs
    args += [norm_g.reshape(1, d)] + w_args
    if ple is not None:
        p_all, ple_g, w_gate, w_proj, post_g = ple
        dp = p_all.shape[-1]
        in_specs += [pl.BlockSpec((pl.Squeezed(), tm, dp), lambda i: (layer, i, 0)), _const_spec((1, d)),
                     _const_spec((d, d)), _const_spec((dp, d)), _const_spec((1, d))]
        args += [p_all, ple_g.reshape(1, d), w_gate.astype(BF16), w_proj.astype(BF16), post_g.reshape(1, d)]
    return pl.pallas_call(
        functools.partial(_ffn_kernel, has_pre=pre is not None, has_ple=ple is not None),
        grid=(t // tm,),
        in_specs=in_specs,
        out_specs=pl.BlockSpec((tm, d), row),
        out_shape=jax.ShapeDtypeStruct((t, d), F32),
        scratch_shapes=[pltpu.VMEM((tm, d_ff), BF16)],
        compiler_params=_params("parallel"),
        name="ffn" + ("_pre" if pre is not None else "") + ("_ple" if ple is not None else ""),
    )(*args)


def _ssd_in_kernel(h_ref, prev_ref, next_ref, g_ref, gather_ref, tail_ref, wz_ref, wx_ref, wdt_ref, dtb_ref,
                   cw_ref, cb_ref, z_ref, xs_ref, c_ref, bt_ref, dt_ref, src_ref, proj_ref, *, tiles_per_seq):
    tm = h_ref.shape[0]
    q = SSD_CHUNK
    n_sub = tm // q
    pos = lax.rem(pl.program_id(0), tiles_per_seq)

    src_ref[0:CONV_HALO, :] = jnp.where(pos > 0, prev_ref[...], 0.0)
    src_ref[CONV_HALO:CONV_HALO + tm, :] = h_ref[...]
    src_ref[CONV_HALO + tm:, :] = jnp.where(pos < tiles_per_seq - 1, next_ref[...], 0.0)
    xn = _rms(src_ref[...], g_ref[...]).astype(BF16)
    conv_rows, plain_rows = [], []
    for ci in range(n_sub):
        picked = _dot(gather_ref[...], xn[ci * q:(ci + 1) * q + 2 * CONV_HALO]).astype(BF16)
        conv_rows.append(picked[0:q])
        plain_rows.append(picked[q:])
    tails = _dot(tail_ref[...], xn).astype(BF16)
    lhs_conv = jnp.concatenate(conv_rows + [tails], axis=0)
    lhs = jnp.concatenate(plain_rows, axis=0)

    n_chunks = wx_ref.shape[0]
    cw = wx_ref.shape[2]
    n_xs = xs_ref.shape[1] // cw
    bcw = c_ref.shape[1] // 2
    n_wrap = SSD_CONV - 1
    z_every = n_chunks * cw // z_ref.shape[1]
    last_sublane = lax.broadcasted_iota(jnp.int32, (SUBLANES, LANES), 0) == SUBLANES - 1
    proj_ref[0] = _dot(lhs_conv, wx_ref[0])
    for c in range(n_chunks):
        if c + 1 < n_chunks:
            proj_ref[(c + 1) % 2] = _dot(lhs_conv, wx_ref[c + 1])
        if c % z_every == z_every - 1:
            zc = (c // z_every) * cw
            z_ref[:, zc:zc + cw] = _dot(lhs, wz_ref[:, zc:zc + cw]).astype(BF16)
        for ci in range(n_sub):
            for lt in range(cw // LANES):
                col = c * cw + lt * LANES
                cur = proj_ref[c % 2, ci * q:(ci + 1) * q, lt * LANES:(lt + 1) * LANES]
                tail = proj_ref[c % 2, tm + SUBLANES * ci:tm + SUBLANES * (ci + 1), lt * LANES:(lt + 1) * LANES]
                wrap = []
                for m in range(n_wrap):
                    up = pltpu.roll(cur[SUBLANES * m:SUBLANES * (m + 1)], SUBLANES - 1, axis=0)
                    end = pltpu.roll(tail, (SUBLANES - 1 - m) % SUBLANES, axis=0)
                    wrap.append(jnp.where(last_sublane, end, up))
                taps = jnp.concatenate([cur] + wrap, axis=0)
                acc = cb_ref[:, col:col + LANES] + taps[0:q] * cw_ref[0:1, col:col + LANES]
                for j in range(1, SSD_CONV):
                    acc = acc + taps[SUBLANES * j:SUBLANES * j + q] * cw_ref[j:j + 1, col:col + LANES]
                out = acc * _sigmoid(acc)
                if c < n_xs:
                    xs_ref[ci * q:(ci + 1) * q, col:col + LANES] = out.astype(BF16)
                else:
                    blk, off = divmod(col - n_xs * cw, bcw)
                    dst = (blk // 2) * bcw + off
                    if blk % 2 == 1:
                        c_ref[ci * q:(ci + 1) * q, dst:dst + LANES] = out.astype(BF16)
                    else:
                        bt_ref[ci, dst:dst + LANES, :] = out.T.astype(BF16)
    raw = _dot(lhs, wdt_ref[...]) + dtb_ref[...]
    dt_ref[...] = jnp.maximum(raw, 0.0) + jnp.log1p(jnp.exp(-jnp.abs(raw)))


def _one_hot_rows(cols, width):
    mat = np.zeros((len(cols), width), np.float32)
    mat[np.arange(len(cols)), cols] = 1.0
    return jnp.asarray(mat, BF16)


def _ssd_in_gather_matrices(tm):
    q = SSD_CHUNK
    half = SSD_CONV // 2
    times = _chunk_times()
    chunk = [CONV_HALO + t - half for t in times] + [CONV_HALO + t for t in times]
    tails = [CONV_HALO + ci * q + q + m - half for ci in range(tm // q) for m in range(SUBLANES)]
    return _one_hot_rows(chunk, q + 2 * CONV_HALO), _one_hot_rows(tails, tm + 2 * CONV_HALO)


def _ssd_in(h, seqlen, norm_g, w_in, dt_bias, conv_w, conv_b, d_inner, conv_dim, tm=512, cw=512):
    t, d = h.shape
    bc2 = (conv_dim - d_inner) // 2
    n_dt = w_in.shape[1] - d_inner - conv_dim
    wz = w_in[:, :d_inner].astype(BF16)
    wx = w_in[:, d_inner:d_inner + conv_dim].reshape(d, conv_dim // cw, cw).transpose(1, 0, 2).astype(BF16)
    wdt = jnp.pad(w_in[:, d_inner + conv_dim:], ((0, 0), (0, LANES - n_dt))).astype(BF16)
    dtb = jnp.pad(dt_bias.reshape(1, n_dt), ((0, 0), (0, LANES - n_dt)))
    cwt = jnp.pad(conv_w, ((0, SUBLANES - SSD_CONV), (0, 0)))
    assert (tm // SSD_CHUNK) % 2 == 0
    gather, tail = _ssd_in_gather_matrices(tm)
    hb = tm // CONV_HALO
    n_hblk = t // CONV_HALO
    row = lambda i: (i, 0)
    return pl.pallas_call(
        functools.partial(_ssd_in_kernel, tiles_per_seq=seqlen // tm),
        grid=(t // tm,),
        in_specs=[pl.BlockSpec((tm, d), row),
                  pl.BlockSpec((CONV_HALO, d), lambda i: (jnp.maximum(i * hb - 1, 0), 0)),
                  pl.BlockSpec((CONV_HALO, d), lambda i: (jnp.minimum((i + 1) * hb, n_hblk - 1), 0)),
                  _const_spec((1, d)), _const_spec(gather.shape), _const_spec(tail.shape), _const_spec(wz.shape),
                  _const_spec(wx.shape),
                  _const_spec(wdt.shape), _const_spec((1, LANES)), _const_spec(cwt.shape),
                  _const_spec((1, conv_dim))],
        out_specs=[pl.BlockSpec((tm, d_inner), row), pl.BlockSpec((tm, d_inner), row), pl.BlockSpec((tm, bc2), row),
                   pl.BlockSpec((tm // SSD_CHUNK, bc2, SSD_CHUNK), lambda i: (i, 0, 0)),
                   pl.BlockSpec((tm, LANES), row)],
        out_shape=[jax.ShapeDtypeStruct((t, d_inner), BF16), jax.ShapeDtypeStruct((t, d_inner), BF16),
                   jax.ShapeDtypeStruct((t, bc2), BF16),
                   jax.ShapeDtypeStruct((t // SSD_CHUNK, bc2, SSD_CHUNK), BF16),
                   jax.ShapeDtypeStruct((t, LANES), F32)],
        scratch_shapes=[pltpu.VMEM((tm + 2 * CONV_HALO, d), F32),
                        pltpu.VMEM((2, tm + SUBLANES * (tm // SSD_CHUNK), cw), F32)],
        compiler_params=_params("parallel"),
        name="ssd_in",
    )(h, h, h, norm_g.reshape(1, d), gather, tail, wz, wx, wdt, dtb, cwt, conv_b.reshape(1, conv_dim))


def _scan_decays(dt_ref, rows, a2, tri, edge_row):
    dt = dt_ref[0, rows, :]
    hi, mid, lo = _split3(dt * a2)
    cum = _dot(tri, hi) + _dot(tri, mid) + _dot(tri, lo)
    dt_dec = dt * jnp.exp2(cum[edge_row:edge_row + 1, :] - cum)
    return cum, (cum - jnp.log2(dt)).T, dt_dec.T


def _scan_chunk(xs_ref, bt_ref, c_ref, y_ref, st_ref, dsk_ref, ci, decays, masks, col0, edge_row):
    q = SSD_CHUNK
    hg = SSD_HEADS // SSD_GROUPS
    gw = hg * SSD_HEAD_DIM
    rows = slice(ci * q, (ci + 1) * q)
    cum, key_t, dd_t = decays
    causal, lane_head, low_half = masks
    for g in range(SSD_GROUPS):
        bg_t = bt_ref[0, ci, g * SSD_D_STATE:(g + 1) * SSD_D_STATE, :]
        cg = c_ref[0, rows, g * SSD_D_STATE:(g + 1) * SSD_D_STATE]
        cb = _dot(cg, bg_t)
        bg_t = bg_t.astype(F32)
        scores, b_scaled, ecum = [], [], []
        for j in range(hg):
            col = col0 + g * hg + j
            cum_q = jnp.broadcast_to(cum[:, col:col + 1], (q, q))
            weight = jnp.exp2(jnp.where(causal, cum_q - key_t[col:col + 1, :], -jnp.inf))
            scores.append((cb * weight).astype(BF16))
            b_scaled.append((bg_t * dd_t[col:col + 1, :]).astype(BF16))
            ecum.append(jnp.exp2(cum_q))
        xg = xs_ref[0, rows, g * gw:(g + 1) * gw]
        x_blocks = jnp.concatenate([jnp.where(lane_head == j, xg, jnp.zeros_like(xg)) for j in range(hg)],
                                   axis=0)
        y_diag = _dot(jnp.concatenate(scores, axis=1), x_blocks)
        ecum_g = jnp.concatenate([jnp.where(low_half, ecum[2 * i], ecum[2 * i + 1]) for i in range(hg // 2)],
                                 axis=1)
        st = st_ref[g]
        y = y_diag + _dot(cg, st.astype(BF16)) * ecum_g
        if dsk_ref is not None:
            y = y + xg.astype(F32) * dsk_ref[:, g * gw:(g + 1) * gw]
        y_ref[0, rows, g * gw:(g + 1) * gw] = y.astype(y_ref.dtype)
        new = _dot(jnp.concatenate(b_scaled, axis=1), x_blocks)
        st_ref[g] = st * ecum_g[edge_row:edge_row + 1, :] + new


def _ssd_scan_kernel(xsf_ref, bf_ref, cf_ref, dtf_ref, xsb_ref, bb_ref, cb_ref, dtb_ref, a_ref, trif_ref, trib_ref,
                     dsk_ref, yf_ref, yb_ref, stf_ref, stb_ref):
    q = SSD_CHUNK
    gw = (SSD_HEADS // SSD_GROUPS) * SSD_HEAD_DIM
    n_sub = xsf_ref.shape[1] // q

    @pl.when(pl.program_id(1) == 0)
    def _():
        stf_ref[...] = jnp.zeros_like(stf_ref)
        stb_ref[...] = jnp.zeros_like(stb_ref)

    def time_of(idx):
        return (q // SUBLANES) * (idx & (SUBLANES - 1)) + (idx >> 3)

    t_row = time_of(lax.broadcasted_iota(jnp.int32, (q, q), 0))
    t_col = time_of(lax.broadcasted_iota(jnp.int32, (q, q), 1))
    lane_head = lax.broadcasted_iota(jnp.int32, (q, gw), 1) // SSD_HEAD_DIM
    low_half = lax.broadcasted_iota(jnp.int32, (q, LANES), 1) < SSD_HEAD_DIM
    masks_f = (t_row >= t_col, lane_head, low_half)
    masks_b = (t_row <= t_col, lane_head, low_half)
    a2 = a_ref[...]
    order_f = list(range(n_sub))
    order_b = order_f[::-1]
    rows_of = lambda ci: slice(ci * q, (ci + 1) * q)
    dec_f = [_scan_decays(dtf_ref, rows_of(ci), a2, trif_ref[...], q - 1) for ci in order_f]
    dec_b = [_scan_decays(dtb_ref, rows_of(ci), a2, trib_ref[...], 0) for ci in order_b]
    for i in range(n_sub):
        _scan_chunk(xsf_ref, bf_ref, cf_ref, yf_ref, stf_ref, dsk_ref, order_f[i], dec_f[i], masks_f, 0, q - 1)
        _scan_chunk(xsb_ref, bb_ref, cb_ref, yb_ref, stb_ref, None, order_b[i], dec_b[i], masks_b, SSD_HEADS, 0)


def _ssd_scan(xs, c, bt, dt, a_log, d_skip):
    bsz, seqlen, d_inner = xs.shape
    n_sub = min(SCAN_CHUNKS_PER_STEP, seqlen // SSD_CHUNK)
    rows = SSD_CHUNK * n_sub
    nb = seqlen // rows
    bcw = SSD_GROUPS * SSD_D_STATE
    a2 = jnp.pad(-jnp.exp(a_log.astype(F32)).reshape(1, 2 * SSD_HEADS) * LOG2E,
                 ((0, 0), (0, LANES - 2 * SSD_HEADS)))
    dsk = jnp.repeat(d_skip.astype(F32), SSD_HEAD_DIM).reshape(1, d_inner)
    times = _chunk_times()
    prefix = (times[None, :] <= times[:, None]).astype(np.float32)
    st_shape = pltpu.VMEM((SSD_GROUPS, SSD_D_STATE, d_inner // SSD_GROUPS), F32)
    fwd = lambda i: i
    bwd = lambda i: nb - 1 - i

    def specs(bidx, direction):
        return [
            pl.BlockSpec((1, rows, d_inner), lambda bi, i: (bi, bidx(i), 0)),
            pl.BlockSpec((1, n_sub, bcw, SSD_CHUNK), lambda bi, i: (bi, bidx(i), direction, 0)),
            pl.BlockSpec((1, rows, bcw), lambda bi, i: (bi, bidx(i), direction)),
            pl.BlockSpec((1, rows, LANES), lambda bi, i: (bi, bidx(i), 0)),
        ]

    y_shape = jax.ShapeDtypeStruct((bsz, seqlen, d_inner), BF16)
    return pl.pallas_call(
        _ssd_scan_kernel,
        grid=(bsz, nb),
        in_specs=specs(fwd, 0) + specs(bwd, 1) + [_const_spec((1, LANES)), _const_spec((SSD_CHUNK, SSD_CHUNK)),
                                                  _const_spec((SSD_CHUNK, SSD_CHUNK)), _const_spec((1, d_inner))],
        out_specs=[pl.BlockSpec((1, rows, d_inner), lambda bi, i: (bi, fwd(i), 0)),
                   pl.BlockSpec((1, rows, d_inner), lambda bi, i: (bi, bwd(i), 0))],
        out_shape=[y_shape, y_shape],
        scratch_shapes=[st_shape, st_shape],
        compiler_params=_params("parallel", "arbitrary"),
        name="ssd_scan",
    )(xs, bt, c, dt, xs, bt, c, dt, a2, jnp.asarray(prefix, BF16), jnp.asarray(prefix.T, BF16), dsk)


def _ssd_out_kernel(h_ref, yf_ref, yb_ref, z_ref, g_ref, order_ref, w_ref, o_ref):
    q = SSD_CHUNK
    z = z_ref[...].astype(F32)
    y = yf_ref[...].astype(F32) + yb_ref[...].astype(F32)
    y = _rms(y * (z * _sigmoid(z)), g_ref[...]).astype(BF16)
    y_time = [_dot(order_ref[...], y[ci * q:(ci + 1) * q]).astype(BF16) for ci in range(y.shape[0] // q)]
    o_ref[...] = h_ref[...] + _dot(jnp.concatenate(y_time, axis=0), w_ref[...].astype(BF16))


def _ssd_out(h, y_f, y_b, z, norm_g, w_out_all, layer, tm=512):
    t, d = h.shape
    d_inner = z.shape[1]
    to_time_order = (_chunk_times()[None, :] == np.arange(SSD_CHUNK)[:, None]).astype(np.float32)
    row = lambda i: (i, 0)
    return pl.pallas_call(
        _ssd_out_kernel,
        grid=(t // tm,),
        in_specs=[pl.BlockSpec((tm, d), row)] + [pl.BlockSpec((tm, d_inner), row)] * 3
                 + [_const_spec((1, d_inner)), _const_spec((SSD_CHUNK, SSD_CHUNK)), _layer_spec(w_out_all, layer)],
        out_specs=pl.BlockSpec((tm, d), row),
        out_shape=jax.ShapeDtypeStruct((t, d), F32),
        compiler_params=_params("parallel"),
        name="ssd_out",
    )(h, y_f, y_b, z, norm_g.reshape(1, d_inner), jnp.asarray(to_time_order, BF16), w_out_all)


def _ssd_mixer(h, bsz, mix_g, w_in, conv_w, conv_b, dt_bias, a_log, d_skip, norm_g, w_out_all, layer):
    t, d = h.shape
    seqlen = t // bsz
    d_inner = SSD_HEADS * SSD_HEAD_DIM
    bcw = SSD_GROUPS * SSD_D_STATE
    conv_dim = d_inner + 4 * bcw
    z, xs, c, bt, dt = _ssd_in(h, seqlen, mix_g, w_in, dt_bias, conv_w, conv_b, d_inner, conv_dim)
    y_f, y_b = _ssd_scan(xs.reshape(bsz, seqlen, d_inner), c.reshape(bsz, seqlen, 2 * bcw),
                         bt.reshape(bsz, seqlen // SSD_CHUNK, 2 * bcw, SSD_CHUNK),
                         dt.reshape(bsz, seqlen, LANES), a_log, d_skip)
    return _ssd_out(h, y_f.reshape(t, d_inner), y_b.reshape(t, d_inner), z, norm_g, w_out_all, layer)


def _na_qkv_kernel(h_ref, g_ref, w_ref, avg_ref, qg_ref, kg_ref, q_ref, k_ref, v_ref, raw_ref, ms_ref):
    xn = _rms(h_ref[...], g_ref[...]).astype(BF16)
    d = q_ref.shape[1]
    gw = avg_ref.shape[0]
    n_groups = 2 * d // gw
    for c in range(n_groups):
        raw_ref[:, c * gw:(c + 1) * gw] = _dot(xn, w_ref[:, c * gw:(c + 1) * gw].astype(BF16))
    v_ref[...] = _dot(xn, w_ref[:, 2 * d:3 * d].astype(BF16)).astype(BF16)
    for c in range(n_groups):
        x = raw_ref[:, c * gw:(c + 1) * gw]
        hi, lo = _split2(x * x)
        ms_ref[:, c * gw:(c + 1) * gw] = _dot(hi, avg_ref[...]) + _dot(lo, avg_ref[...])
    for c in range(n_groups):
        cols = slice(c * gw, (c + 1) * gw)
        x = raw_ref[:, cols] * lax.rsqrt(ms_ref[:, cols] + RMS_EPS)
        if c < n_groups // 2:
            q_ref[:, cols] = (x * qg_ref[:, cols] * (NA_HEAD_DIM ** -0.5 * LOG2E)).astype(BF16)
        else:
            kcols = slice(c * gw - d, (c + 1) * gw - d)
            k_ref[:, kcols] = (x * kg_ref[:, kcols]).astype(BF16)


def _na_qkv(h, norm_g, w_qkv_all, layer, q_norm, k_norm, tm=512, gw=256):
    t, d = h.shape
    n_heads = d // NA_HEAD_DIM
    same_head = np.arange(gw)[:, None] // NA_HEAD_DIM == np.arange(gw)[None, :] // NA_HEAD_DIM
    avg = jnp.asarray(same_head.astype(np.float32) / NA_HEAD_DIM, BF16)
    row = lambda i: (i, 0)
    out = jax.ShapeDtypeStruct((t, d), BF16)
    return pl.pallas_call(
        _na_qkv_kernel,
        grid=(t // tm,),
        in_specs=[pl.BlockSpec((tm, d), row), _const_spec((1, d)), _layer_spec(w_qkv_all, layer), _const_spec((gw, gw)),
                  _const_spec((1, d)), _const_spec((1, d))],
        out_specs=[pl.BlockSpec((tm, d), row)] * 3,
        out_shape=[out, out, out],
        scratch_shapes=[pltpu.VMEM((tm, 2 * d), F32), pltpu.VMEM((tm, 2 * d), F32)],
        compiler_params=_params("parallel"),
        name="na_qkv",
    )(h, norm_g.reshape(1, d), w_qkv_all, avg,
      jnp.tile(q_norm, n_heads).reshape(1, d), jnp.tile(k_norm, n_heads).reshape(1, d))


def _na_attn_kernel(q_ref, k_ref, v_ref, bias_ref, o_ref, s_ref, p_ref, *, rows):
    nk = NA_WIN_ROWS * GRID_W
    n_pairs = q_ref.shape[2] // LANES
    low = lax.broadcasted_iota(jnp.int32, (GRID_W, LANES), 1) < NA_HEAD_DIM
    starts, first_rel = [], []
    for i in range(NA_ROWS_PER_STEP):
        r = pl.program_id(1) * NA_ROWS_PER_STEP + i
        r0 = jnp.clip(r - NA_WIN_ROWS // 2, 0, rows - NA_WIN_ROWS)
        starts.append(pl.multiple_of(r0 * GRID_W, GRID_W))
        first_rel.append(r0 - r + NA_WIN_ROWS - 1)
    units = [(i, hp) for i in range(NA_ROWS_PER_STEP) for hp in range(n_pairs)]

    row_max = []
    for u, (i, hp) in enumerate(units):
        q2 = q_ref[0, i * GRID_W:(i + 1) * GRID_W, hp * LANES:(hp + 1) * LANES]
        zero = jnp.zeros_like(q2)
        qab = jnp.concatenate([jnp.where(low, q2, zero), jnp.where(low, zero, q2)], axis=0)
        k2 = k_ref[0, pl.ds(starts[i], nk), hp * LANES:(hp + 1) * LANES]
        s = lax.dot_general(qab, k2, (((1,), (1,)), ((), ())), preferred_element_type=F32)
        bias = jnp.concatenate(
            [jnp.concatenate([bias_ref[2 * hp, first_rel[i] + 2 * j], bias_ref[2 * hp + 1, first_rel[i] + 2 * j]],
                             axis=0) for j in range(NA_WIN_ROWS // 2)], axis=1)
        s = s + bias
        s_ref[u] = s
        row_max.append(jnp.max(s, axis=-1, keepdims=True))
    inv_sum = []
    for u in range(len(units)):
        e = jnp.exp2(s_ref[u] - row_max[u])
        inv_sum.append(1.0 / jnp.sum(e, axis=-1, keepdims=True))
        p_ref[u] = e.astype(BF16)
    for u, (i, hp) in enumerate(units):
        v2 = v_ref[0, pl.ds(starts[i], nk), hp * LANES:(hp + 1) * LANES]
        o2 = _dot(p_ref[u], v2) * inv_sum[u]
        o_ref[0, i * GRID_W:(i + 1) * GRID_W, hp * LANES:(hp + 1) * LANES] = (
            jnp.where(low, o2[:GRID_W], o2[GRID_W:]).astype(o_ref.dtype))


def _na_bias_kernel(rpb_ref, place_ref, mask_ref, o_ref):
    n_rel_cols = 2 * NA_WIN_COLS - 1
    n_rel_rows = 2 * NA_WIN_ROWS - 1
    base = pl.program_id(0) * (n_rel_rows * n_rel_cols)
    for r in range(n_rel_rows - 1):
        acc = mask_ref[...]
        for c in range(n_rel_cols):
            acc = acc + rpb_ref[base + r * n_rel_cols + c] * place_ref[c]
            acc = acc + rpb_ref[base + (r + 1) * n_rel_cols + c] * place_ref[n_rel_cols + c]
        o_ref[0, r] = acc


def _na_bias_tiles(rpb):
    n_heads, n_rel_rows, n_rel_cols = rpb.shape
    qc = np.arange(GRID_W)[:, None]
    kc = np.arange(GRID_W)[None, :]
    win_c0 = np.clip(qc - NA_WIN_COLS // 2, 0, GRID_W - NA_WIN_COLS)
    valid = (kc >= win_c0) & (kc < win_c0 + NA_WIN_COLS)
    place = np.zeros((2 * n_rel_cols, GRID_W, LANES), np.float32)
    for c in range(n_rel_cols):
        hit = ((kc - qc + NA_WIN_COLS - 1 == c) & valid).astype(np.float32)
        place[c, :, :GRID_W] = hit
        place[n_rel_cols + c, :, GRID_W:] = hit
    mask = np.tile(np.where(valid, 0.0, MASK_VALUE).astype(np.float32), (1, 2))
    return pl.pallas_call(
        _na_bias_kernel,
        grid=(n_heads,),
        in_specs=[pl.BlockSpec(memory_space=pltpu.SMEM), _const_spec(place.shape), _const_spec(mask.shape)],
        out_specs=pl.BlockSpec((1, n_rel_rows - 1, GRID_W, LANES), lambda hd: (hd, 0, 0, 0)),
        out_shape=jax.ShapeDtypeStruct((n_heads, n_rel_rows - 1, GRID_W, LANES), F32),
        compiler_params=_params("parallel"),
        name="na_bias",
    )((rpb.astype(F32) * LOG2E).reshape(-1), jnp.asarray(place), jnp.asarray(mask))


def _na_attn(q, k, v, rpb):
    bsz, seqlen, d = q.shape
    rows = seqlen // GRID_W
    assert rows >= NA_WIN_ROWS and rows % NA_ROWS_PER_STEP == 0
    bias = _na_bias_tiles(rpb)
    n_units = NA_ROWS_PER_STEP * (d // LANES)
    blk = NA_ROWS_PER_STEP * GRID_W
    return pl.pallas_call(
        functools.partial(_na_attn_kernel, rows=rows),
        grid=(bsz, rows // NA_ROWS_PER_STEP),
        in_specs=[pl.BlockSpec((1, blk, d), lambda bi, r: (bi, r, 0)),
                  pl.BlockSpec((1, seqlen, d), lambda bi, r: (bi, 0, 0)),
                  pl.BlockSpec((1, seqlen, d), lambda bi, r: (bi, 0, 0)),
                  _const_spec(bias.shape)],
        out_specs=pl.BlockSpec((1, blk, d), lambda bi, r: (bi, r, 0)),
        out_shape=jax.ShapeDtypeStruct((bsz, seqlen, d), BF16),
        scratch_shapes=[pltpu.VMEM((n_units, 2 * GRID_W, NA_WIN_ROWS * GRID_W), F32),
                        pltpu.VMEM((n_units, 2 * GRID_W, NA_WIN_ROWS * GRID_W), BF16)],
        compiler_params=_params("parallel", "arbitrary"),
        name="na_attn",
    )(q, k, v, bias)


def _na_mixer(h, bsz, mix_g, w_qkv_all, layer, q_norm, k_norm, rpb):
    t, d = h.shape
    seqlen = t // bsz
    q, k, v = _na_qkv(h, mix_g, w_qkv_all, layer, q_norm, k_norm)
    shp = (bsz, seqlen, d)
    return _na_attn(q.reshape(shp), k.reshape(shp), v.reshape(shp), rpb).reshape(t, d)


def kernel(x, p, ffn1_norm, ffn1_w_gu, ffn1_w_down, mix_norm, ffn2_norm, ffn2_w_gu, ffn2_w_down, ple_norm, ple_w_gate, ple_w_proj, ple_post_norm, ssd_w_in, ssd_conv_w, ssd_conv_b, ssd_dt_bias, ssd_a_log, ssd_d, ssd_norm, ssd_w_out, na_w_qkv, na_q_norm, na_k_norm, na_rpb, na_w_out):
    bsz, seqlen, d = x.shape
    depth = p.shape[0]
    t = bsz * seqlen
    h = x.reshape(t, d)
    p_all = p.reshape(depth, t, p.shape[-1])
    for i in range(depth):
        h = _ffn(h, ffn1_norm[i], ffn1_w_gu, ffn1_w_down, i)
        j = i // 2
        pre = None
        if i % 2 == 0:
            h = _ssd_mixer(h, bsz, mix_norm[i], ssd_w_in[j], ssd_conv_w[j], ssd_conv_b[j], ssd_dt_bias[j],
                           ssd_a_log[j], ssd_d[j], ssd_norm[j], ssd_w_out, j)
        else:
            pre = (_na_mixer(h, bsz, mix_norm[i], na_w_qkv, j, na_q_norm[j], na_k_norm[j], na_rpb[j]), na_w_out[j])
        h = _ffn(h, ffn2_norm[i], ffn2_w_gu, ffn2_w_down, i, pre=pre,
                 ple=(p_all, ple_norm[i], ple_w_gate[i], ple_w_proj[i], ple_post_norm[i]))
    return h.reshape(bsz, seqlen, d)
```

```python
import functools
import math

import jax
import jax.numpy as jnp
import numpy as np
from jax import lax
from jax.experimental import pallas as pl
from jax.experimental.pallas import tpu as pltpu

F32 = jnp.float32
BF16 = jnp.bfloat16

RMS_EPS = 1e-6
SSD_HEAD_DIM = 64
SSD_HEADS = 32
SSD_GROUPS = 8
SSD_D_STATE = 128
SSD_CONV = 5
SSD_CHUNK = 128
GRID_W = 64
NA_HEAD_DIM = 64
NA_WIN_ROWS = 8
NA_WIN_COLS = 16
NA_ROWS_PER_STEP = 2
FFN_CHUNK = 256
SUBLANES = 8
LANES = 128
CONV_HALO = SUBLANES
SCAN_CHUNKS_PER_STEP = 4
VMEM_LIMIT = 56 * 1024 * 1024
MASK_VALUE = -1e30
LOG2E = math.log2(math.e)


def _params(*sem):
    return pltpu.CompilerParams(dimension_semantics=sem, vmem_limit_bytes=VMEM_LIMIT)


def _const_spec(shape):
    nd = len(shape)
    return pl.BlockSpec(shape, lambda *_: (0,) * nd, pipeline_mode=pl.Buffered(1))


def _layer_spec(stacked, layer):
    nd = stacked.ndim - 1
    return pl.BlockSpec((pl.Squeezed(),) + stacked.shape[1:], lambda *_: (layer,) + (0,) * nd,
                        pipeline_mode=pl.Buffered(1))


def _rms(x, g):
    ms = jnp.mean(x * x, axis=-1, keepdims=True)
    return x * lax.rsqrt(ms + RMS_EPS) * g


def _sigmoid(x):
    return 1.0 / (1.0 + jnp.exp(-x))


def _dot(a, b):
    return jnp.dot(a, b, preferred_element_type=F32)


def _split2(x):
    hi = x.astype(BF16)
    lo = (x - hi.astype(F32)).astype(BF16)
    return hi, lo


def _split3(x):
    hi = x.astype(BF16)
    r = x - hi.astype(F32)
    mid = r.astype(BF16)
    lo = (r - mid.astype(F32)).astype(BF16)
    return hi, mid, lo


def _chunk_times():
    p = np.arange(SSD_CHUNK)
    return (SSD_CHUNK // SUBLANES) * (p % SUBLANES) + p // SUBLANES


def _ffn_kernel(*refs, has_pre, has_ple):
    refs = list(refs)
    h_ref = refs.pop(0)
    x = h_ref[...]
    if has_pre:
        pre_ref, wpre_ref = refs.pop(0), refs.pop(0)
        x = x + _dot(pre_ref[...], wpre_ref[...])
    g_ref, wgu_ref, wd_ref = refs.pop(0), refs.pop(0), refs.pop(0)
    o_ref, a_ref = refs[-2:]
    xn = _rms(x, g_ref[...]).astype(BF16)
    d_ff = wd_ref.shape[0]
    for c in range(d_ff // FFN_CHUNK):
        gate = _dot(xn, wgu_ref[:, c * FFN_CHUNK:(c + 1) * FFN_CHUNK].astype(BF16))
        up = _dot(xn, wgu_ref[:, d_ff + c * FFN_CHUNK:d_ff + (c + 1) * FFN_CHUNK].astype(BF16))
        a_ref[:, c * FFN_CHUNK:(c + 1) * FFN_CHUNK] = (gate * _sigmoid(gate) * up).astype(BF16)
    h2 = x + 0.5 * _dot(a_ref[...], wd_ref[...].astype(BF16))
    if has_ple:
        p_ref, pg_ref, wgate_ref, wproj_ref, ppg_ref = refs[:5]
        gate = _sigmoid(_dot(_rms(h2, pg_ref[...]).astype(BF16), wgate_ref[...]))
        emb = _dot(p_ref[...].astype(BF16), wproj_ref[...])
        h2 = h2 + gate * _rms(emb, ppg_ref[...])
    o_ref[...] = h2


def _ffn(h, norm_g, w_gu_all, w_down_all, layer, pre=None, ple=None, tm=512):
    t, d = h.shape
    d_ff = w_down_all.shape[1]
    assert d_ff % FFN_CHUNK == 0
    row = lambda i: (i, 0)
    in_specs = [pl.BlockSpec((tm, d), row)]
    args = [h]
    if pre is not None:
        x_pre, w_pre = pre
        in_specs += [pl.BlockSpec((tm, x_pre.shape[1]), row), _const_spec(w_pre.shape)]
        args += [x_pre, w_pre.astype(BF16)]
    in_specs += [_const_spec((1, d)), _layer_spec(w_gu_all, layer), _layer_spec(w_down_all, layer)]
    args += [norm_g.reshape(1, d), w_gu_all, w_down_all]
    if ple is not None:
        p_all, ple_g, w_gate, w_proj, post_g = ple
        dp = p_all.shape[-1]
        in_specs += [pl.BlockSpec((pl.Squeezed(), tm, dp), lambda i: (layer, i, 0)), _const_spec((1, d)),
                     _const_spec((d, d)), _const_spec((dp, d)), _const_spec((1, d))]
        args += [p_all, ple_g.reshape(1, d), w_gate.astype(BF16), w_proj.astype(BF16), post_g.reshape(1, d)]
    return pl.pallas_call(
        functools.partial(_ffn_kernel, has_pre=pre is not None, has_ple=ple is not None),
        grid=(t // tm,),
        in_specs=in_specs,
        out_specs=pl.BlockSpec((tm, d), row),
        out_shape=jax.ShapeDtypeStruct((t, d), F32),
        scratch_shapes=[pltpu.VMEM((tm, d_ff), BF16)],
        compiler_params=_params("parallel"),
        name="ffn" + ("_pre" if pre is not None else "") + ("_ple" if ple is not None else ""),
    )(*args)


def _ssd_in_kernel(h_ref, prev_ref, next_ref, g_ref, gather_ref, tail_ref, wz_ref, wx_ref, wdt_ref, dtb_ref,
                   cw_ref, cb_ref, z_ref, xs_ref, c_ref, bt_ref, dt_ref, src_ref, proj_ref, *, tiles_per_seq):
    tm = h_ref.shape[0]
    q = SSD_CHUNK
    n_sub = tm // q
    pos = lax.rem(pl.program_id(0), tiles_per_seq)

    src_ref[0:CONV_HALO, :] = jnp.where(pos > 0, prev_ref[...], 0.0)
    src_ref[CONV_HALO:CONV_HALO + tm, :] = h_ref[...]
    src_ref[CONV_HALO + tm:, :] = jnp.where(pos < tiles_per_seq - 1, next_ref[...], 0.0)
    xn = _rms(src_ref[...], g_ref[...]).astype(BF16)
    conv_rows, plain_rows = [], []
    for ci in range(n_sub):
        picked = _dot(gather_ref[...], xn[ci * q:(ci + 1) * q + 2 * CONV_HALO]).astype(BF16)
        conv_rows.append(picked[0:q])
        plain_rows.append(picked[q:])
    tails = _dot(tail_ref[...], xn).astype(BF16)
    lhs_conv = jnp.concatenate(conv_rows + [tails], axis=0)
    lhs = jnp.concatenate(plain_rows, axis=0)

    n_chunks = wx_ref.shape[0]
    cw = wx_ref.shape[2]
    n_xs = xs_ref.shape[1] // cw
    bcw = c_ref.shape[1] // 2
    n_wrap = SSD_CONV - 1
    z_every = n_chunks * cw // z_ref.shape[1]
    last_sublane = lax.broadcasted_iota(jnp.int32, (SUBLANES, LANES), 0) == SUBLANES - 1
    proj_ref[0] = _dot(lhs_conv, wx_ref[0])
    for c in range(n_chunks):
        if c + 1 < n_chunks:
            proj_ref[(c + 1) % 2] = _dot(lhs_conv, wx_ref[c + 1])
        if c % z_every == z_every - 1:
            zc = (c // z_every) * cw
            z_ref[:, zc:zc + cw] = _dot(lhs, wz_ref[:, zc:zc + cw]).astype(BF16)
        for ci in range(n_sub):
            for lt in range(cw // LANES):
                col = c * cw + lt * LANES
                cur = proj_ref[c % 2, ci * q:(ci + 1) * q, lt * LANES:(lt + 1) * LANES]
                tail = proj_ref[c % 2, tm + SUBLANES * ci:tm + SUBLANES * (ci + 1), lt * LANES:(lt + 1) * LANES]
                wrap = []
                for m in range(n_wrap):
                    up = pltpu.roll(cur[SUBLANES * m:SUBLANES * (m + 1)], SUBLANES - 1, axis=0)
                    end = pltpu.roll(tail, (SUBLANES - 1 - m) % SUBLANES, axis=0)
                    wrap.append(jnp.where(last_sublane, end, up))
                taps = jnp.concatenate([cur] + wrap, axis=0)
                acc = cb_ref[:, col:col + LANES] + taps[0:q] * cw_ref[0:1, col:col + LANES]
                for j in range(1, SSD_CONV):
                    acc = acc + taps[SUBLANES * j:SUBLANES * j + q] * cw_ref[j:j + 1, col:col + LANES]
                out = acc * _sigmoid(acc)
                if c < n_xs:
                    xs_ref[ci * q:(ci + 1) * q, col:col + LANES] = out.astype(BF16)
                else:
                    blk, off = divmod(col - n_xs * cw, bcw)
                    dst = (blk // 2) * bcw + off
                    if blk % 2 == 1:
                        c_ref[ci * q:(ci + 1) * q, dst:dst + LANES] = out.astype(BF16)
                    else:
                        bt_ref[ci, dst:dst + LANES, :] = out.T.astype(BF16)
    raw = _dot(lhs, wdt_ref[...]) + dtb_ref[...]
    dt_ref[...] = jnp.maximum(raw, 0.0) + jnp.log1p(jnp.exp(-jnp.abs(raw)))


def _one_hot_rows(cols, width):
    mat = np.zeros((len(cols), width), np.float32)
    mat[np.arange(len(cols)), cols] = 1.0
    return jnp.asarray(mat, BF16)


def _ssd_in_gather_matrices(tm):
    q = SSD_CHUNK
    half = SSD_CONV // 2
    times = _chunk_times()
    chunk = [CONV_HALO + t - half for t in times] + [CONV_HALO + t for t in times]
    tails = [CONV_HALO + ci * q + q + m - half for ci in range(tm // q) for m in range(SUBLANES)]
    return _one_hot_rows(chunk, q + 2 * CONV_HALO), _one_hot_rows(tails, tm + 2 * CONV_HALO)


def _ssd_in(h, seqlen, norm_g, w_in, dt_bias, conv_w, conv_b, d_inner, conv_dim, tm=512, cw=512):
    t, d = h.shape
    bc2 = (conv_dim - d_inner) // 2
    n_dt = w_in.shape[1] - d_inner - conv_dim
    wz = w_in[:, :d_inner].astype(BF16)
    wx = w_in[:, d_inner:d_inner + conv_dim].reshape(d, conv_dim // cw, cw).transpose(1, 0, 2).astype(BF16)
    wdt = jnp.pad(w_in[:, d_inner + conv_dim:], ((0, 0), (0, LANES - n_dt))).astype(BF16)
    dtb = jnp.pad(dt_bias.reshape(1, n_dt), ((0, 0), (0, LANES - n_dt)))
    cwt = jnp.pad(conv_w, ((0, SUBLANES - SSD_CONV), (0, 0)))
    assert (tm // SSD_CHUNK) % 2 == 0
    gather, tail = _ssd_in_gather_matrices(tm)
    hb = tm // CONV_HALO
    n_hblk = t // CONV_HALO
    row = lambda i: (i, 0)
    return pl.pallas_call(
        functools.partial(_ssd_in_kernel, tiles_per_seq=seqlen // tm),
        grid=(t // tm,),
        in_specs=[pl.BlockSpec((tm, d), row),
                  pl.BlockSpec((CONV_HALO, d), lambda i: (jnp.maximum(i * hb - 1, 0), 0)),
                  pl.BlockSpec((CONV_HALO, d), lambda i: (jnp.minimum((i + 1) * hb, n_hblk - 1), 0)),
                  _const_spec((1, d)), _const_spec(gather.shape), _const_spec(tail.shape), _const_spec(wz.shape),
                  _const_spec(wx.shape),
                  _const_spec(wdt.shape), _const_spec((1, LANES)), _const_spec(cwt.shape),
                  _const_spec((1, conv_dim))],
        out_specs=[pl.BlockSpec((tm, d_inner), row), pl.BlockSpec((tm, d_inner), row), pl.BlockSpec((tm, bc2), row),
                   pl.BlockSpec((tm // SSD_CHUNK, bc2, SSD_CHUNK), lambda i: (i, 0, 0)),
                   pl.BlockSpec((tm, LANES), row)],
        out_shape=[jax.ShapeDtypeStruct((t, d_inner), BF16), jax.ShapeDtypeStruct((t, d_inner), BF16),
                   jax.ShapeDtypeStruct((t, bc2), BF16),
                   jax.ShapeDtypeStruct((t // SSD_CHUNK, bc2, SSD_CHUNK), BF16),
                   jax.ShapeDtypeStruct((t, LANES), F32)],
        scratch_shapes=[pltpu.VMEM((tm + 2 * CONV_HALO, d), F32),
                        pltpu.VMEM((2, tm + SUBLANES * (tm // SSD_CHUNK), cw), F32)],
        compiler_params=_params("parallel"),
        name="ssd_in",
    )(h, h, h, norm_g.reshape(1, d), gather, tail, wz, wx, wdt, dtb, cwt, conv_b.reshape(1, conv_dim))


def _scan_decays(dt_ref, rows, a2, tri, edge_row):
    dt = dt_ref[0, rows, :]
    hi, mid, lo = _split3(dt * a2)
    cum = _dot(tri, hi) + _dot(tri, mid) + _dot(tri, lo)
    dt_dec = dt * jnp.exp2(cum[edge_row:edge_row + 1, :] - cum)
    return cum, (cum - jnp.log2(dt)).T, dt_dec.T


def _scan_chunk(xs_ref, bt_ref, c_ref, y_ref, st_ref, dsk_ref, ci, decays, masks, col0, edge_row):
    q = SSD_CHUNK
    hg = SSD_HEADS // SSD_GROUPS
    gw = hg * SSD_HEAD_DIM
    rows = slice(ci * q, (ci + 1) * q)
    cum, key_t, dd_t = decays
    causal, lane_head, low_half = masks
    for g in range(SSD_GROUPS):
        bg_t = bt_ref[0, ci, g * SSD_D_STATE:(g + 1) * SSD_D_STATE, :]
        cg = c_ref[0, rows, g * SSD_D_STATE:(g + 1) * SSD_D_STATE]
        cb = _dot(cg, bg_t)
        bg_t = bg_t.astype(F32)
        scores, b_scaled, ecum = [], [], []
        for j in range(hg):
            col = col0 + g * hg + j
            cum_q = jnp.broadcast_to(cum[:, col:col + 1], (q, q))
            weight = jnp.exp2(jnp.where(causal, cum_q - key_t[col:col + 1, :], -jnp.inf))
            scores.append((cb * weight).astype(BF16))
            b_scaled.append((bg_t * dd_t[col:col + 1, :]).astype(BF16))
            ecum.append(jnp.exp2(cum_q))
        xg = xs_ref[0, rows, g * gw:(g + 1) * gw]
        x_blocks = jnp.concatenate([jnp.where(lane_head == j, xg, jnp.zeros_like(xg)) for j in range(hg)],
                                   axis=0)
        y_diag = _dot(jnp.concatenate(scores, axis=1), x_blocks)
        ecum_g = jnp.concatenate([jnp.where(low_half, ecum[2 * i], ecum[2 * i + 1]) for i in range(hg // 2)],
                                 axis=1)
        st = st_ref[g]
        y = y_diag + _dot(cg, st.astype(BF16)) * ecum_g
        if dsk_ref is not None:
            y = y + xg.astype(F32) * dsk_ref[:, g * gw:(g + 1) * gw]
        y_ref[0, rows, g * gw:(g + 1) * gw] = y.astype(y_ref.dtype)
        new = _dot(jnp.concatenate(b_scaled, axis=1), x_blocks)
        st_ref[g] = st * ecum_g[edge_row:edge_row + 1, :] + new


def _ssd_scan_kernel(xsf_ref, bf_ref, cf_ref, dtf_ref, xsb_ref, bb_ref, cb_ref, dtb_ref, a_ref, trif_ref, trib_ref,
                     dsk_ref, yf_ref, yb_ref, stf_ref, stb_ref):
    q = SSD_CHUNK
    gw = (SSD_HEADS // SSD_GROUPS) * SSD_HEAD_DIM
    n_sub = xsf_ref.shape[1] // q

    @pl.when(pl.program_id(1) == 0)
    def _():
        stf_ref[...] = jnp.zeros_like(stf_ref)
        stb_ref[...] = jnp.zeros_like(stb_ref)

    def time_of(idx):
        return (q // SUBLANES) * (idx & (SUBLANES - 1)) + (idx >> 3)

    t_row = time_of(lax.broadcasted_iota(jnp.int32, (q, q), 0))
    t_col = time_of(lax.broadcasted_iota(jnp.int32, (q, q), 1))
    lane_head = lax.broadcasted_iota(jnp.int32, (q, gw), 1) // SSD_HEAD_DIM
    low_half = lax.broadcasted_iota(jnp.int32, (q, LANES), 1) < SSD_HEAD_DIM
    masks_f = (t_row >= t_col, lane_head, low_half)
    masks_b = (t_row <= t_col, lane_head, low_half)
    a2 = a_ref[...]
    order_f = list(range(n_sub))
    order_b = order_f[::-1]
    rows_of = lambda ci: slice(ci * q, (ci + 1) * q)
    dec_f = [_scan_decays(dtf_ref, rows_of(ci), a2, trif_ref[...], q - 1) for ci in order_f]
    dec_b = [_scan_decays(dtb_ref, rows_of(ci), a2, trib_ref[...], 0) for ci in order_b]
    for i in range(n_sub):
        _scan_chunk(xsf_ref, bf_ref, cf_ref, yf_ref, stf_ref, dsk_ref, order_f[i], dec_f[i], masks_f, 0, q - 1)
        _scan_chunk(xsb_ref, bb_ref, cb_ref, yb_ref, stb_ref, None, order_b[i], dec_b[i], masks_b, SSD_HEADS, 0)


def _ssd_scan(xs, c, bt, dt, a_log, d_skip):
    bsz, seqlen, d_inner = xs.shape
    n_sub = min(SCAN_CHUNKS_PER_STEP, seqlen // SSD_CHUNK)
    rows = SSD_CHUNK * n_sub
    nb = seqlen // rows
    bcw = SSD_GROUPS * SSD_D_STATE
    a2 = jnp.pad(-jnp.exp(a_log.astype(F32)).reshape(1, 2 * SSD_HEADS) * LOG2E,
                 ((0, 0), (0, LANES - 2 * SSD_HEADS)))
    dsk = jnp.repeat(d_skip.astype(F32), SSD_HEAD_DIM).reshape(1, d_inner)
    times = _chunk_times()
    prefix = (times[None, :] <= times[:, None]).astype(np.float32)
    st_shape = pltpu.VMEM((SSD_GROUPS, SSD_D_STATE, d_inner // SSD_GROUPS), F32)
    fwd = lambda i: i
    bwd = lambda i: nb - 1 - i

    def specs(bidx, direction):
        return [
            pl.BlockSpec((1, rows, d_inner), lambda bi, i: (bi, bidx(i), 0)),
            pl.BlockSpec((1, n_sub, bcw, SSD_CHUNK), lambda bi, i: (bi, bidx(i), direction, 0)),
            pl.BlockSpec((1, rows, bcw), lambda bi, i: (bi, bidx(i), direction)),
            pl.BlockSpec((1, rows, LANES), lambda bi, i: (bi, bidx(i), 0)),
        ]

    y_shape = jax.ShapeDtypeStruct((bsz, seqlen, d_inner), BF16)
    return pl.pallas_call(
        _ssd_scan_kernel,
        grid=(bsz, nb),
        in_specs=specs(fwd, 0) + specs(bwd, 1) + [_const_spec((1, LANES)), _const_spec((SSD_CHUNK, SSD_CHUNK)),
                                                  _const_spec((SSD_CHUNK, SSD_CHUNK)), _const_spec((1, d_inner))],
        out_specs=[pl.BlockSpec((1, rows, d_inner), lambda bi, i: (bi, fwd(i), 0)),
                   pl.BlockSpec((1, rows, d_inner), lambda bi, i: (bi, bwd(i), 0))],
        out_shape=[y_shape, y_shape],
        scratch_shapes=[st_shape, st_shape],
        compiler_params=_params("parallel", "arbitrary"),
        name="ssd_scan",
    )(xs, bt, c, dt, xs, bt, c, dt, a2, jnp.asarray(prefix, BF16), jnp.asarray(prefix.T, BF16), dsk)


def _ssd_out_kernel(h_ref, yf_ref, yb_ref, z_ref, g_ref, order_ref, w_ref, o_ref):
    q = SSD_CHUNK
    z = z_ref[...].astype(F32)
    y = yf_ref[...].astype(F32) + yb_ref[...].astype(F32)
    y = _rms(y * (z * _sigmoid(z)), g_ref[...]).astype(BF16)
    y_time = [_dot(order_ref[...], y[ci * q:(ci + 1) * q]).astype(BF16) for ci in range(y.shape[0] // q)]
    o_ref[...] = h_ref[...] + _dot(jnp.concatenate(y_time, axis=0), w_ref[...].astype(BF16))


def _ssd_out(h, y_f, y_b, z, norm_g, w_out_all, layer, tm=512):
    t, d = h.shape
    d_inner = z.shape[1]
    to_time_order = (_chunk_times()[None, :] == np.arange(SSD_CHUNK)[:, None]).astype(np.float32)
    row = lambda i: (i, 0)
    return pl.pallas_call(
        _ssd_out_kernel,
        grid=(t // tm,),
        in_specs=[pl.BlockSpec((tm, d), row)] + [pl.BlockSpec((tm, d_inner), row)] * 3
                 + [_const_spec((1, d_inner)), _const_spec((SSD_CHUNK, SSD_CHUNK)), _layer_spec(w_out_all, layer)],
        out_specs=pl.BlockSpec((tm, d), row),
        out_shape=jax.ShapeDtypeStruct((t, d), F32),
        compiler_params=_params("parallel"),
        name="ssd_out",
    )(h, y_f, y_b, z, norm_g.reshape(1, d_inner), jnp.asarray(to_time_order, BF16), w_out_all)


def _ssd_mixer(h, bsz, mix_g, w_in, conv_w, conv_b, dt_bias, a_log, d_skip, norm_g, w_out_all, layer):
    t, d = h.shape
    seqlen = t // bsz
    d_inner = SSD_HEADS * SSD_HEAD_DIM
    bcw = SSD_GROUPS * SSD_D_STATE
    conv_dim = d_inner + 4 * bcw
    z, xs, c, bt, dt = _ssd_in(h, seqlen, mix_g, w_in, dt_bias, conv_w, conv_b, d_inner, conv_dim)
    y_f, y_b = _ssd_scan(xs.reshape(bsz, seqlen, d_inner), c.reshape(bsz, seqlen, 2 * bcw),
                         bt.reshape(bsz, seqlen // SSD_CHUNK, 2 * bcw, SSD_CHUNK),
                         dt.reshape(bsz, seqlen, LANES), a_log, d_skip)
    return _ssd_out(h, y_f.reshape(t, d_inner), y_b.reshape(t, d_inner), z, norm_g, w_out_all, layer)


def _na_qkv_kernel(h_ref, g_ref, w_ref, avg_ref, qg_ref, kg_ref, q_ref, k_ref, v_ref, raw_ref, ms_ref):
    xn = _rms(h_ref[...], g_ref[...]).astype(BF16)
    d = q_ref.shape[1]
    gw = avg_ref.shape[0]
    n_groups = 2 * d // gw
    for c in range(n_groups):
        raw_ref[:, c * gw:(c + 1) * gw] = _dot(xn, w_ref[:, c * gw:(c + 1) * gw].astype(BF16))
    v_ref[...] = _dot(xn, w_ref[:, 2 * d:3 * d].astype(BF16)).astype(BF16)
    for c in range(n_groups):
        x = raw_ref[:, c * gw:(c + 1) * gw]
        hi, lo = _split2(x * x)
        ms_ref[:, c * gw:(c + 1) * gw] = _dot(hi, avg_ref[...]) + _dot(lo, avg_ref[...])
    for c in range(n_groups):
        cols = slice(c * gw, (c + 1) * gw)
        x = raw_ref[:, cols] * lax.rsqrt(ms_ref[:, cols] + RMS_EPS)
        if c < n_groups // 2:
            q_ref[:, cols] = (x * qg_ref[:, cols] * (NA_HEAD_DIM ** -0.5 * LOG2E)).astype(BF16)
        else:
            kcols = slice(c * gw - d, (c + 1) * gw - d)
            k_ref[:, kcols] = (x * kg_ref[:, kcols]).astype(BF16)


def _na_qkv(h, norm_g, w_qkv_all, layer, q_norm, k_norm, tm=512, gw=256):
    t, d = h.shape
    n_heads = d // NA_HEAD_DIM
    same_head = np.arange(gw)[:, None] // NA_HEAD_DIM == np.arange(gw)[None, :] // NA_HEAD_DIM
    avg = jnp.asarray(same_head.astype(np.float32) / NA_HEAD_DIM, BF16)
    row = lambda i: (i, 0)
    out = jax.ShapeDtypeStruct((t, d), BF16)
    return pl.pallas_call(
        _na_qkv_kernel,
        grid=(t // tm,),
        in_specs=[pl.BlockSpec((tm, d), row), _const_spec((1, d)), _layer_spec(w_qkv_all, layer), _const_spec((gw, gw)),
                  _const_spec((1, d)), _const_spec((1, d))],
        out_specs=[pl.BlockSpec((tm, d), row)] * 3,
        out_shape=[out, out, out],
        scratch_shapes=[pltpu.VMEM((tm, 2 * d), F32), pltpu.VMEM((tm, 2 * d), F32)],
        compiler_params=_params("parallel"),
        name="na_qkv",
    )(h, norm_g.reshape(1, d), w_qkv_all, avg,
      jnp.tile(q_norm, n_heads).reshape(1, d), jnp.tile(k_norm, n_heads).reshape(1, d))


def _na_attn_kernel(q_ref, k_ref, v_ref, bias_ref, o_ref, s_ref, p_ref, *, rows):
    nk = NA_WIN_ROWS * GRID_W
    n_pairs = q_ref.shape[2] // LANES
    low = lax.broadcasted_iota(jnp.int32, (GRID_W, LANES), 1) < NA_HEAD_DIM
    starts, first_rel = [], []
    for i in range(NA_ROWS_PER_STEP):
        r = pl.program_id(1) * NA_ROWS_PER_STEP + i
        r0 = jnp.clip(r - NA_WIN_ROWS // 2, 0, rows - NA_WIN_ROWS)
        starts.append(pl.multiple_of(r0 * GRID_W, GRID_W))
        first_rel.append(r0 - r + NA_WIN_ROWS - 1)
    units = [(i, hp) for i in range(NA_ROWS_PER_STEP) for hp in range(n_pairs)]

    row_max = []
    for u, (i, hp) in enumerate(units):
        q2 = q_ref[0, i * GRID_W:(i + 1) * GRID_W, hp * LANES:(hp + 1) * LANES]
        zero = jnp.zeros_like(q2)
        qab = jnp.concatenate([jnp.where(low, q2, zero), jnp.where(low, zero, q2)], axis=0)
        k2 = k_ref[0, pl.ds(starts[i], nk), hp * LANES:(hp + 1) * LANES]
        s = lax.dot_general(qab, k2, (((1,), (1,)), ((), ())), preferred_element_type=F32)
        bias = jnp.concatenate(
            [jnp.concatenate([bias_ref[2 * hp, first_rel[i] + 2 * j], bias_ref[2 * hp + 1, first_rel[i] + 2 * j]],
                             axis=0) for j in range(NA_WIN_ROWS // 2)], axis=1)
        s = s + bias
        s_ref[u] = s
        row_max.append(jnp.max(s, axis=-1, keepdims=True))
    inv_sum = []
    for u in range(len(units)):
        e = jnp.exp2(s_ref[u] - row_max[u])
        inv_sum.append(1.0 / jnp.sum(e, axis=-1, keepdims=True))
        p_ref[u] = e.astype(BF16)
    for u, (i, hp) in enumerate(units):
        v2 = v_ref[0, pl.ds(starts[i], nk), hp * LANES:(hp + 1) * LANES]
        o2 = _dot(p_ref[u], v2) * inv_sum[u]
        o_ref[0, i * GRID_W:(i + 1) * GRID_W, hp * LANES:(hp + 1) * LANES] = (
            jnp.where(low, o2[:GRID_W], o2[GRID_W:]).astype(o_ref.dtype))


def _na_bias_kernel(rpb_ref, place_ref, mask_ref, o_ref):
    n_rel_cols = 2 * NA_WIN_COLS - 1
    n_rel_rows = 2 * NA_WIN_ROWS - 1
    base = pl.program_id(0) * (n_rel_rows * n_rel_cols)
    for r in range(n_rel_rows - 1):
        acc = mask_ref[...]
        for c in range(n_rel_cols):
            acc = acc + rpb_ref[base + r * n_rel_cols + c] * place_ref[c]
            acc = acc + rpb_ref[base + (r + 1) * n_rel_cols + c] * place_ref[n_rel_cols + c]
        o_ref[0, r] = acc


def _na_bias_tiles(rpb):
    n_heads, n_rel_rows, n_rel_cols = rpb.shape
    qc = np.arange(GRID_W)[:, None]
    kc = np.arange(GRID_W)[None, :]
    win_c0 = np.clip(qc - NA_WIN_COLS // 2, 0, GRID_W - NA_WIN_COLS)
    valid = (kc >= win_c0) & (kc < win_c0 + NA_WIN_COLS)
    place = np.zeros((2 * n_rel_cols, GRID_W, LANES), np.float32)
    for c in range(n_rel_cols):
        hit = ((kc - qc + NA_WIN_COLS - 1 == c) & valid).astype(np.float32)
        place[c, :, :GRID_W] = hit
        place[n_rel_cols + c, :, GRID_W:] = hit
    mask = np.tile(np.where(valid, 0.0, MASK_VALUE).astype(np.float32), (1, 2))
    return pl.pallas_call(
        _na_bias_kernel,
        grid=(n_heads,),
        in_specs=[pl.BlockSpec(memory_space=pltpu.SMEM), _const_spec(place.shape), _const_spec(mask.shape)],
        out_specs=pl.BlockSpec((1, n_rel_rows - 1, GRID_W, LANES), lambda hd: (hd, 0, 0, 0)),
        out_shape=jax.ShapeDtypeStruct((n_heads, n_rel_rows - 1, GRID_W, LANES), F32),
        compiler_params=_params("parallel"),
        name="na_bias",
    )((rpb.astype(F32) * LOG2E).reshape(-1), jnp.asarray(place), jnp.asarray(mask))


def _na_attn(q, k, v, rpb):
    bsz, seqlen, d = q.shape
    rows = seqlen // GRID_W
    assert rows >= NA_WIN_ROWS and rows % NA_ROWS_PER_STEP == 0
    bias = _na_bias_tiles(rpb)
    n_units = NA_ROWS_PER_STEP * (d // LANES)
    blk = NA_ROWS_PER_STEP * GRID_W
    return pl.pallas_call(
        functools.partial(_na_attn_kernel, rows=rows),
        grid=(bsz, rows // NA_ROWS_PER_STEP),
        in_specs=[pl.BlockSpec((1, blk, d), lambda bi, r: (bi, r, 0)),
                  pl.BlockSpec((1, seqlen, d), lambda bi, r: (bi, 0, 0)),
                  pl.BlockSpec((1, seqlen, d), lambda bi, r: (bi, 0, 0)),
                  _const_spec(bias.shape)],
        out_specs=pl.BlockSpec((1, blk, d), lambda bi, r: (bi, r, 0)),
        out_shape=jax.ShapeDtypeStruct((bsz, seqlen, d), BF16),
        scratch_shapes=[pltpu.VMEM((n_units, 2 * GRID_W, NA_WIN_ROWS * GRID_W), F32),
                        pltpu.VMEM((n_units, 2 * GRID_W, NA_WIN_ROWS * GRID_W), BF16)],
        compiler_params=_params("parallel", "arbitrary"),
        name="na_attn",
    )(q, k, v, bias)


def _na_mixer(h, bsz, mix_g, w_qkv_all, layer, q_norm, k_norm, rpb):
    t, d = h.shape
    seqlen = t // bsz
    q, k, v = _na_qkv(h, mix_g, w_qkv_all, layer, q_norm, k_norm)
    shp = (bsz, seqlen, d)
    return _na_attn(q.reshape(shp), k.reshape(shp), v.reshape(shp), rpb).reshape(t, d)


def kernel(x, p, ffn1_norm, ffn1_w_gu, ffn1_w_down, mix_norm, ffn2_norm, ffn2_w_gu, ffn2_w_down, ple_norm, ple_w_gate, ple_w_proj, ple_post_norm, ssd_w_in, ssd_conv_w, ssd_conv_b, ssd_dt_bias, ssd_a_log, ssd_d, ssd_norm, ssd_w_out, na_w_qkv, na_q_norm, na_k_norm, na_rpb, na_w_out):
    bsz, seqlen, d = x.shape
    depth = p.shape[0]
    t = bsz * seqlen
    h = x.reshape(t, d)
    p_all = p.reshape(depth, t, p.shape[-1])
    for i in range(depth):
        h = _ffn(h, ffn1_norm[i], ffn1_w_gu, ffn1_w_down, i)
        j = i // 2
        pre = None
        if i % 2 == 0:
            h = _ssd_mixer(h, bsz, mix_norm[i], ssd_w_in[j], ssd_conv_w[j], ssd_conv_b[j], ssd_dt_bias[j],
                           ssd_a_log[j], ssd_d[j], ssd_norm[j], ssd_w_out, j)
        else:
            pre = (_na_mixer(h, bsz, mix_norm[i], na_w_qkv, j, na_q_norm[j], na_k_norm[j], na_rpb[j]), na_w_out[j])
        h = _ffn(h, ffn2_norm[i], ffn2_w_gu, ffn2_w_down, i, pre=pre,
                 ple=(p_all, ple_norm[i], ple_w_gate[i], ple_w_proj[i], ple_post_norm[i]))
    return h.reshape(bsz, seqlen, d)
```

```python
import functools
import math

import jax
import jax.numpy as jnp
import numpy as np
from jax import lax
from jax.experimental import pallas as pl
from jax.experimental.pallas import tpu as pltpu

F32 = jnp.float32
BF16 = jnp.bfloat16

RMS_EPS = 1e-6
SSD_HEAD_DIM = 64
SSD_HEADS = 32
SSD_GROUPS = 8
SSD_D_STATE = 128
SSD_CONV = 5
SSD_CHUNK = 128
GRID_W = 64
NA_HEAD_DIM = 64
NA_WIN_ROWS = 8
NA_WIN_COLS = 16
NA_ROWS_PER_STEP = 2
FFN_CHUNK = 256
SUBLANES = 8
LANES = 128
CONV_HALO = SUBLANES
SCAN_CHUNKS_PER_STEP = 4
VMEM_LIMIT = 56 * 1024 * 1024
MASK_VALUE = -1e30
LOG2E = math.log2(math.e)


def _params(*sem):
    return pltpu.CompilerParams(dimension_semantics=sem, vmem_limit_bytes=VMEM_LIMIT)


def _const_spec(shape):
    nd = len(shape)
    return pl.BlockSpec(shape, lambda *_: (0,) * nd, pipeline_mode=pl.Buffered(1))


def _layer_spec(stacked, layer):
    nd = stacked.ndim - 1
    return pl.BlockSpec((pl.Squeezed(),) + stacked.shape[1:], lambda *_: (layer,) + (0,) * nd,
                        pipeline_mode=pl.Buffered(1))


def _rms(x, g):
    ms = jnp.mean(x * x, axis=-1, keepdims=True)
    return x * lax.rsqrt(ms + RMS_EPS) * g


def _sigmoid(x):
    return 1.0 / (1.0 + jnp.exp(-x))


def _dot(a, b):
    return jnp.dot(a, b, preferred_element_type=F32)


def _split2(x):
    hi = x.astype(BF16)
    lo = (x - hi.astype(F32)).astype(BF16)
    return hi, lo


def _split3(x):
    hi = x.astype(BF16)
    r = x - hi.astype(F32)
    mid = r.astype(BF16)
    lo = (r - mid.astype(F32)).astype(BF16)
    return hi, mid, lo


def _chunk_times():
    p = np.arange(SSD_CHUNK)
    return (SSD_CHUNK // SUBLANES) * (p % SUBLANES) + p // SUBLANES


def _ffn_kernel(*refs, has_pre, has_ple):
    refs = list(refs)
    h_ref = refs.pop(0)
    x = h_ref[...]
    if has_pre:
        pre_ref, wpre_ref = refs.pop(0), refs.pop(0)
        x = x + _dot(pre_ref[...], wpre_ref[...])
    g_ref, wgu_ref, wd_ref = refs.pop(0), refs.pop(0), refs.pop(0)
    o_ref, a_ref = refs[-2:]
    xn = _rms(x, g_ref[...]).astype(BF16)
    d_ff = wd_ref.shape[0]
    for c in range(d_ff // FFN_CHUNK):
        gate = _dot(xn, wgu_ref[:, c * FFN_CHUNK:(c + 1) * FFN_CHUNK].astype(BF16))
        up = _dot(xn, wgu_ref[:, d_ff + c * FFN_CHUNK:d_ff + (c + 1) * FFN_CHUNK].astype(BF16))
        a_ref[:, c * FFN_CHUNK:(c + 1) * FFN_CHUNK] = (gate * _sigmoid(gate) * up).astype(BF16)
    h2 = x + 0.5 * _dot(a_ref[...], wd_ref[...].astype(BF16))
    if has_ple:
        p_ref, pg_ref, wgate_ref, wproj_ref, ppg_ref = refs[:5]
        gate = _sigmoid(_dot(_rms(h2, pg_ref[...]).astype(BF16), wgate_ref[...]))
        emb = _dot(p_ref[...].astype(BF16), wproj_ref[...])
        h2 = h2 + gate * _rms(emb, ppg_ref[...])
    o_ref[...] = h2


def _ffn(h, norm_g, w_gu_all, w_down_all, layer, pre=None, ple=None, tm=512):
    t, d = h.shape
    d_ff = w_down_all.shape[1]
    assert d_ff % FFN_CHUNK == 0
    row = lambda i: (i, 0)
    in_specs = [pl.BlockSpec((tm, d), row)]
    args = [h]
    if pre is not None:
        x_pre, w_pre = pre
        in_specs += [pl.BlockSpec((tm, x_pre.shape[1]), row), _const_spec(w_pre.shape)]
        args += [x_pre, w_pre.astype(BF16)]
    in_specs += [_const_spec((1, d)), _layer_spec(w_gu_all, layer), _layer_spec(w_down_all, layer)]
    args += [norm_g.reshape(1, d), w_gu_all, w_down_all]
    if ple is not None:
        p_all, ple_g, w_gate, w_proj, post_g = ple
        dp = p_all.shape[-1]
        in_specs += [pl.BlockSpec((pl.Squeezed(), tm, dp), lambda i: (layer, i, 0)), _const_spec((1, d)),
                     _const_spec((d, d)), _const_spec((dp, d)), _const_spec((1, d))]
        args += [p_all, ple_g.reshape(1, d), w_gate.astype(BF16), w_proj.astype(BF16), post_g.reshape(1, d)]
    return pl.pallas_call(
        functools.partial(_ffn_kernel, has_pre=pre is not None, has_ple=ple is not None),
        grid=(t // tm,),
        in_specs=in_specs,
        out_specs=pl.BlockSpec((tm, d), row),
        out_shape=jax.ShapeDtypeStruct((t, d), F32),
        scratch_shapes=[pltpu.VMEM((tm, d_ff), BF16)],
        compiler_params=_params("parallel"),
        name="ffn" + ("_pre" if pre is not None else "") + ("_ple" if ple is not None else ""),
    )(*args)


def _ssd_in_kernel(h_ref, prev_ref, next_ref, g_ref, gather_ref, tail_ref, wz_ref, wx_ref, wdt_ref, dtb_ref,
                   cw_ref, cb_ref, z_ref, xs_ref, c_ref, bt_ref, dt_ref, src_ref, proj_ref, *, tiles_per_seq):
    tm = h_ref.shape[0]
    q = SSD_CHUNK
    n_sub = tm // q
    pos = lax.rem(pl.program_id(0), tiles_per_seq)

    src_ref[0:CONV_HALO, :] = jnp.where(pos > 0, prev_ref[...], 0.0)
    src_ref[CONV_HALO:CONV_HALO + tm, :] = h_ref[...]
    src_ref[CONV_HALO + tm:, :] = jnp.where(pos < tiles_per_seq - 1, next_ref[...], 0.0)
    xn = _rms(src_ref[...], g_ref[...]).astype(BF16)
    conv_rows, plain_rows = [], []
    for ci in range(n_sub):
        picked = _dot(gather_ref[...], xn[ci * q:(ci + 1) * q + 2 * CONV_HALO]).astype(BF16)
        conv_rows.append(picked[0:q])
        plain_rows.append(picked[q:])
    tails = _dot(tail_ref[...], xn).astype(BF16)
    lhs_conv = jnp.concatenate(conv_rows + [tails], axis=0)
    lhs = jnp.concatenate(plain_rows, axis=0)

    n_chunks = wx_ref.shape[0]
    cw = wx_ref.shape[2]
    n_xs = xs_ref.shape[1] // cw
    bcw = c_ref.shape[1] // 2
    n_wrap = SSD_CONV - 1
    z_every = n_chunks * cw // z_ref.shape[1]
    last_sublane = lax.broadcasted_iota(jnp.int32, (SUBLANES, LANES), 0) == SUBLANES - 1
    proj_ref[0] = _dot(lhs_conv, wx_ref[0])
    for c in range(n_chunks):
        if c + 1 < n_chunks:
            proj_ref[(c + 1) % 2] = _dot(lhs_conv, wx_ref[c + 1])
        if c % z_every == z_every - 1:
            zc = (c // z_every) * cw
            z_ref[:, zc:zc + cw] = _dot(lhs, wz_ref[:, zc:zc + cw]).astype(BF16)
        for ci in range(n_sub):
            for lt in range(cw // LANES):
                col = c * cw + lt * LANES
                cur = proj_ref[c % 2, ci * q:(ci + 1) * q, lt * LANES:(lt + 1) * LANES]
                tail = proj_ref[c % 2, tm + SUBLANES * ci:tm + SUBLANES * (ci + 1), lt * LANES:(lt + 1) * LANES]
                wrap = []
                for m in range(n_wrap):
                    up = pltpu.roll(cur[SUBLANES * m:SUBLANES * (m + 1)], SUBLANES - 1, axis=0)
                    end = pltpu.roll(tail, (SUBLANES - 1 - m) % SUBLANES, axis=0)
                    wrap.append(jnp.where(last_sublane, end, up))
                taps = jnp.concatenate([cur] + wrap, axis=0)
                acc = cb_ref[:, col:col + LANES] + taps[0:q] * cw_ref[0:1, col:col + LANES]
                for j in range(1, SSD_CONV):
                    acc = acc + taps[SUBLANES * j:SUBLANES * j + q] * cw_ref[j:j + 1, col:col + LANES]
                out = acc * _sigmoid(acc)
                if c < n_xs:
                    xs_ref[ci * q:(ci + 1) * q, col:col + LANES] = out.astype(BF16)
                else:
                    blk, off = divmod(col - n_xs * cw, bcw)
                    dst = (blk // 2) * bcw + off
                    if blk % 2 == 1:
                        c_ref[ci * q:(ci + 1) * q, dst:dst + LANES] = out.astype(BF16)
                    else:
                        bt_ref[ci, dst:dst + LANES, :] = out.T.astype(BF16)
    raw = _dot(lhs, wdt_ref[...]) + dtb_ref[...]
    dt_ref[...] = jnp.maximum(raw, 0.0) + jnp.log1p(jnp.exp(-jnp.abs(raw)))


def _one_hot_rows(cols, width):
    mat = np.zeros((len(cols), width), np.float32)
    mat[np.arange(len(cols)), cols] = 1.0
    return jnp.asarray(mat, BF16)


def _ssd_in_gather_matrices(tm):
    q = SSD_CHUNK
    half = SSD_CONV // 2
    times = _chunk_times()
    chunk = [CONV_HALO + t - half for t in times] + [CONV_HALO + t for t in times]
    tails = [CONV_HALO + ci * q + q + m - half for ci in range(tm // q) for m in range(SUBLANES)]
    return _one_hot_rows(chunk, q + 2 * CONV_HALO), _one_hot_rows(tails, tm + 2 * CONV_HALO)


def _ssd_in(h, seqlen, norm_g, w_in, dt_bias, conv_w, conv_b, d_inner, conv_dim, tm=512, cw=512):
    t, d = h.shape
    bc2 = (conv_dim - d_inner) // 2
    n_dt = w_in.shape[1] - d_inner - conv_dim
    wz = w_in[:, :d_inner].astype(BF16)
    wx = w_in[:, d_inner:d_inner + conv_dim].reshape(d, conv_dim // cw, cw).transpose(1, 0, 2).astype(BF16)
    wdt = jnp.pad(w_in[:, d_inner + conv_dim:], ((0, 0), (0, LANES - n_dt))).astype(BF16)
    dtb = jnp.pad(dt_bias.reshape(1, n_dt), ((0, 0), (0, LANES - n_dt)))
    cwt = jnp.pad(conv_w, ((0, SUBLANES - SSD_CONV), (0, 0)))
    assert (tm // SSD_CHUNK) % 2 == 0
    gather, tail = _ssd_in_gather_matrices(tm)
    hb = tm // CONV_HALO
    n_hblk = t // CONV_HALO
    row = lambda i: (i, 0)
    return pl.pallas_call(
        functools.partial(_ssd_in_kernel, tiles_per_seq=seqlen // tm),
        grid=(t // tm,),
        in_specs=[pl.BlockSpec((tm, d), row),
                  pl.BlockSpec((CONV_HALO, d), lambda i: (jnp.maximum(i * hb - 1, 0), 0)),
                  pl.BlockSpec((CONV_HALO, d), lambda i: (jnp.minimum((i + 1) * hb, n_hblk - 1), 0)),
                  _const_spec((1, d)), _const_spec(gather.shape), _const_spec(tail.shape), _const_spec(wz.shape),
                  _const_spec(wx.shape),
                  _const_spec(wdt.shape), _const_spec((1, LANES)), _const_spec(cwt.shape),
                  _const_spec((1, conv_dim))],
        out_specs=[pl.BlockSpec((tm, d_inner), row), pl.BlockSpec((tm, d_inner), row), pl.BlockSpec((tm, bc2), row),
                   pl.BlockSpec((tm // SSD_CHUNK, bc2, SSD_CHUNK), lambda i: (i, 0, 0)),
                   pl.BlockSpec((tm, LANES), row)],
        out_shape=[jax.ShapeDtypeStruct((t, d_inner), BF16), jax.ShapeDtypeStruct((t, d_inner), BF16),
                   jax.ShapeDtypeStruct((t, bc2), BF16),
                   jax.ShapeDtypeStruct((t // SSD_CHUNK, bc2, SSD_CHUNK), BF16),
                   jax.ShapeDtypeStruct((t, LANES), F32)],
        scratch_shapes=[pltpu.VMEM((tm + 2 * CONV_HALO, d), F32),
                        pltpu.VMEM((2, tm + SUBLANES * (tm // SSD_CHUNK), cw), F32)],
        compiler_params=_params("parallel"),
        name="ssd_in",
    )(h, h, h, norm_g.reshape(1, d), gather, tail, wz, wx, wdt, dtb, cwt, conv_b.reshape(1, conv_dim))


def _scan_decays(dt_ref, rows, a2, tri, edge_row):
    dt = dt_ref[0, rows, :]
    hi, mid, lo = _split3(dt * a2)
    cum = _dot(tri, hi) + _dot(tri, mid) + _dot(tri, lo)
    dt_dec = dt * jnp.exp2(cum[edge_row:edge_row + 1, :] - cum)
    return cum, (cum - jnp.log2(dt)).T, dt_dec.T


def _scan_chunk(xs_ref, bt_ref, c_ref, y_ref, st_ref, dsk_ref, ci, decays, masks, col0, edge_row):
    q = SSD_CHUNK
    hg = SSD_HEADS // SSD_GROUPS
    gw = hg * SSD_HEAD_DIM
    rows = slice(ci * q, (ci + 1) * q)
    cum, key_t, dd_t = decays
    causal, lane_head, low_half = masks
    for g in range(SSD_GROUPS):
        bg_t = bt_ref[0, ci, g * SSD_D_STATE:(g + 1) * SSD_D_STATE, :]
        cg = c_ref[0, rows, g * SSD_D_STATE:(g + 1) * SSD_D_STATE]
        cb = _dot(cg, bg_t)
        bg_t = bg_t.astype(F32)
        scores, b_scaled, ecum = [], [], []
        for j in range(hg):
            col = col0 + g * hg + j
            cum_q = jnp.broadcast_to(cum[:, col:col + 1], (q, q))
            weight = jnp.exp2(jnp.where(causal, cum_q - key_t[col:col + 1, :], -jnp.inf))
            scores.append((cb * weight).astype(BF16))
            b_scaled.append((bg_t * dd_t[col:col + 1, :]).astype(BF16))
            ecum.append(jnp.exp2(cum_q))
        xg = xs_ref[0, rows, g * gw:(g + 1) * gw]
        x_blocks = jnp.concatenate([jnp.where(lane_head == j, xg, jnp.zeros_like(xg)) for j in range(hg)],
                                   axis=0)
        y_diag = _dot(jnp.concatenate(scores, axis=1), x_blocks)
        ecum_g = jnp.concatenate([jnp.where(low_half, ecum[2 * i], ecum[2 * i + 1]) for i in range(hg // 2)],
                                 axis=1)
        st = st_ref[g]
        y = y_diag + _dot(cg, st.astype(BF16)) * ecum_g
        if dsk_ref is not None:
            y = y + xg.astype(F32) * dsk_ref[:, g * gw:(g + 1) * gw]
        y_ref[0, rows, g * gw:(g + 1) * gw] = y.astype(y_ref.dtype)
        new = _dot(jnp.concatenate(b_scaled, axis=1), x_blocks)
        st_ref[g] = st * ecum_g[edge_row:edge_row + 1, :] + new


def _ssd_scan_kernel(xsf_ref, bf_ref, cf_ref, dtf_ref, xsb_ref, bb_ref, cb_ref, dtb_ref, a_ref, trif_ref, trib_ref,
                     dsk_ref, yf_ref, yb_ref, stf_ref, stb_ref):
    q = SSD_CHUNK
    gw = (SSD_HEADS // SSD_GROUPS) * SSD_HEAD_DIM
    n_sub = xsf_ref.shape[1] // q

    @pl.when(pl.program_id(1) == 0)
    def _():
        stf_ref[...] = jnp.zeros_like(stf_ref)
        stb_ref[...] = jnp.zeros_like(stb_ref)

    def time_of(idx):
        return (q // SUBLANES) * (idx & (SUBLANES - 1)) + (idx >> 3)

    t_row = time_of(lax.broadcasted_iota(jnp.int32, (q, q), 0))
    t_col = time_of(lax.broadcasted_iota(jnp.int32, (q, q), 1))
    lane_head = lax.broadcasted_iota(jnp.int32, (q, gw), 1) // SSD_HEAD_DIM
    low_half = lax.broadcasted_iota(jnp.int32, (q, LANES), 1) < SSD_HEAD_DIM
    masks_f = (t_row >= t_col, lane_head, low_half)
    masks_b = (t_row <= t_col, lane_head, low_half)
    a2 = a_ref[...]
    order_f = list(range(n_sub))
    order_b = order_f[::-1]
    rows_of = lambda ci: slice(ci * q, (ci + 1) * q)
    dec_f = [_scan_decays(dtf_ref, rows_of(ci), a2, trif_ref[...], q - 1) for ci in order_f]
    dec_b = [_scan_decays(dtb_ref, rows_of(ci), a2, trib_ref[...], 0) for ci in order_b]
    for i in range(n_sub):
        _scan_chunk(xsf_ref, bf_ref, cf_ref, yf_ref, stf_ref, dsk_ref, order_f[i], dec_f[i], masks_f, 0, q - 1)
        _scan_chunk(xsb_ref, bb_ref, cb_ref, yb_ref, stb_ref, None, order_b[i], dec_b[i], masks_b, SSD_HEADS, 0)


def _ssd_scan(xs, c, bt, dt, a_log, d_skip):
    bsz, seqlen, d_inner = xs.shape
    n_sub = min(SCAN_CHUNKS_PER_STEP, seqlen // SSD_CHUNK)
    rows = SSD_CHUNK * n_sub
    nb = seqlen // rows
    bcw = SSD_GROUPS * SSD_D_STATE
    a2 = jnp.pad(-jnp.exp(a_log.astype(F32)).reshape(1, 2 * SSD_HEADS) * LOG2E,
                 ((0, 0), (0, LANES - 2 * SSD_HEADS)))
    dsk = jnp.repeat(d_skip.astype(F32), SSD_HEAD_DIM).reshape(1, d_inner)
    times = _chunk_times()
    prefix = (times[None, :] <= times[:, None]).astype(np.float32)
    st_shape = pltpu.VMEM((SSD_GROUPS, SSD_D_STATE, d_inner // SSD_GROUPS), F32)
    fwd = lambda i: i
    bwd = lambda i: nb - 1 - i

    def specs(bidx, direction):
        return [
            pl.BlockSpec((1, rows, d_inner), lambda bi, i: (bi, bidx(i), 0)),
            pl.BlockSpec((1, n_sub, bcw, SSD_CHUNK), lambda bi, i: (bi, bidx(i), direction, 0)),
            pl.BlockSpec((1, rows, bcw), lambda bi, i: (bi, bidx(i), direction)),
            pl.BlockSpec((1, rows, LANES), lambda bi, i: (bi, bidx(i), 0)),
        ]

    y_shape = jax.ShapeDtypeStruct((bsz, seqlen, d_inner), BF16)
    return pl.pallas_call(
        _ssd_scan_kernel,
        grid=(bsz, nb),
        in_specs=specs(fwd, 0) + specs(bwd, 1) + [_const_spec((1, LANES)), _const_spec((SSD_CHUNK, SSD_CHUNK)),
                                                  _const_spec((SSD_CHUNK, SSD_CHUNK)), _const_spec((1, d_inner))],
        out_specs=[pl.BlockSpec((1, rows, d_inner), lambda bi, i: (bi, fwd(i), 0)),
                   pl.BlockSpec((1, rows, d_inner), lambda bi, i: (bi, bwd(i), 0))],
        out_shape=[y_shape, y_shape],
        scratch_shapes=[st_shape, st_shape],
        compiler_params=_params("parallel", "arbitrary"),
        name="ssd_scan",
    )(xs, bt, c, dt, xs, bt, c, dt, a2, jnp.asarray(prefix, BF16), jnp.asarray(prefix.T, BF16), dsk)


def _ssd_out_kernel(h_ref, yf_ref, yb_ref, z_ref, g_ref, order_ref, w_ref, o_ref):
    q = SSD_CHUNK
    z = z_ref[...].astype(F32)
    y = yf_ref[...].astype(F32) + yb_ref[...].astype(F32)
    y = _rms(y * (z * _sigmoid(z)), g_ref[...]).astype(BF16)
    y_time = [_dot(order_ref[...], y[ci * q:(ci + 1) * q]).astype(BF16) for ci in range(y.shape[0] // q)]
    o_ref[...] = h_ref[...] + _dot(jnp.concatenate(y_time, axis=0), w_ref[...].astype(BF16))


def _ssd_out(h, y_f, y_b, z, norm_g, w_out_all, layer, tm=512):
    t, d = h.shape
    d_inner = z.shape[1]
    to_time_order = (_chunk_times()[None, :] == np.arange(SSD_CHUNK)[:, None]).astype(np.float32)
    row = lambda i: (i, 0)
    return pl.pallas_call(
        _ssd_out_kernel,
        grid=(t // tm,),
        in_specs=[pl.BlockSpec((tm, d), row)] + [pl.BlockSpec((tm, d_inner), row)] * 3
                 + [_const_spec((1, d_inner)), _const_spec((SSD_CHUNK, SSD_CHUNK)), _layer_spec(w_out_all, layer)],
        out_specs=pl.BlockSpec((tm, d), row),
        out_shape=jax.ShapeDtypeStruct((t, d), F32),
        compiler_params=_params("parallel"),
        name="ssd_out",
    )(h, y_f, y_b, z, norm_g.reshape(1, d_inner), jnp.asarray(to_time_order, BF16), w_out_all)


def _ssd_mixer(h, bsz, mix_g, w_in, conv_w, conv_b, dt_bias, a_log, d_skip, norm_g, w_out_all, layer):
    t, d = h.shape
    seqlen = t // bsz
    d_inner = SSD_HEADS * SSD_HEAD_DIM
    bcw = SSD_GROUPS * SSD_D_STATE
    conv_dim = d_inner + 4 * bcw
    z, xs, c, bt, dt = _ssd_in(h, seqlen, mix_g, w_in, dt_bias, conv_w, conv_b, d_inner, conv_dim)
    y_f, y_b = _ssd_scan(xs.reshape(bsz, seqlen, d_inner), c.reshape(bsz, seqlen, 2 * bcw),
                         bt.reshape(bsz, seqlen // SSD_CHUNK, 2 * bcw, SSD_CHUNK),
                         dt.reshape(bsz, seqlen, LANES), a_log, d_skip)
    return _ssd_out(h, y_f.reshape(t, d_inner), y_b.reshape(t, d_inner), z, norm_g, w_out_all, layer)


def _na_qkv_kernel(h_ref, g_ref, w_ref, avg_ref, qg_ref, kg_ref, q_ref, k_ref, v_ref, raw_ref, ms_ref):
    xn = _rms(h_ref[...], g_ref[...]).astype(BF16)
    d = q_ref.shape[1]
    gw = avg_ref.shape[0]
    n_groups = 2 * d // gw
    for c in range(n_groups):
        raw_ref[:, c * gw:(c + 1) * gw] = _dot(xn, w_ref[:, c * gw:(c + 1) * gw].astype(BF16))
    v_ref[...] = _dot(xn, w_ref[:, 2 * d:3 * d].astype(BF16)).astype(BF16)
    for c in range(n_groups):
        x = raw_ref[:, c * gw:(c + 1) * gw]
        ms_ref[:, c * gw:(c + 1) * gw] = _dot((x * x).astype(BF16), avg_ref[...])
    for c in range(n_groups):
        cols = slice(c * gw, (c + 1) * gw)
        x = raw_ref[:, cols] * lax.rsqrt(ms_ref[:, cols] + RMS_EPS)
        if c < n_groups // 2:
            q_ref[:, cols] = (x * qg_ref[:, cols] * (NA_HEAD_DIM ** -0.5 * LOG2E)).astype(BF16)
        else:
            kcols = slice(c * gw - d, (c + 1) * gw - d)
            k_ref[:, kcols] = (x * kg_ref[:, kcols]).astype(BF16)


def _na_qkv(h, norm_g, w_qkv_all, layer, q_norm, k_norm, tm=512, gw=256):
    t, d = h.shape
    n_heads = d // NA_HEAD_DIM
    same_head = np.arange(gw)[:, None] // NA_HEAD_DIM == np.arange(gw)[None, :] // NA_HEAD_DIM
    avg = jnp.asarray(same_head.astype(np.float32) / NA_HEAD_DIM, BF16)
    row = lambda i: (i, 0)
    out = jax.ShapeDtypeStruct((t, d), BF16)
    return pl.pallas_call(
        _na_qkv_kernel,
        grid=(t // tm,),
        in_specs=[pl.BlockSpec((tm, d), row), _const_spec((1, d)), _layer_spec(w_qkv_all, layer), _const_spec((gw, gw)),
                  _const_spec((1, d)), _const_spec((1, d))],
        out_specs=[pl.BlockSpec((tm, d), row)] * 3,
        out_shape=[out, out, out],
        scratch_shapes=[pltpu.VMEM((tm, 2 * d), F32), pltpu.VMEM((tm, 2 * d), F32)],
        compiler_params=_params("parallel"),
        name="na_qkv",
    )(h, norm_g.reshape(1, d), w_qkv_all, avg,
      jnp.tile(q_norm, n_heads).reshape(1, d), jnp.tile(k_norm, n_heads).reshape(1, d))


def _na_attn_kernel(q_ref, k_ref, v_ref, bias_ref, o_ref, s_ref, p_ref, *, rows):
    nk = NA_WIN_ROWS * GRID_W
    n_pairs = q_ref.shape[2] // LANES
    low = lax.broadcasted_iota(jnp.int32, (GRID_W, LANES), 1) < NA_HEAD_DIM
    starts, first_rel = [], []
    for i in range(NA_ROWS_PER_STEP):
        r = pl.program_id(1) * NA_ROWS_PER_STEP + i
        r0 = jnp.clip(r - NA_WIN_ROWS // 2, 0, rows - NA_WIN_ROWS)
        starts.append(pl.multiple_of(r0 * GRID_W, GRID_W))
        first_rel.append(r0 - r + NA_WIN_ROWS - 1)
    units = [(i, hp) for i in range(NA_ROWS_PER_STEP) for hp in range(n_pairs)]

    row_max = []
    for u, (i, hp) in enumerate(units):
        q2 = q_ref[0, i * GRID_W:(i + 1) * GRID_W, hp * LANES:(hp + 1) * LANES]
        zero = jnp.zeros_like(q2)
        qab = jnp.concatenate([jnp.where(low, q2, zero), jnp.where(low, zero, q2)], axis=0)
        k2 = k_ref[0, pl.ds(starts[i], nk), hp * LANES:(hp + 1) * LANES]
        s = lax.dot_general(qab, k2, (((1,), (1,)), ((), ())), preferred_element_type=F32)
        bias = jnp.concatenate(
            [jnp.concatenate([bias_ref[2 * hp, first_rel[i] + 2 * j], bias_ref[2 * hp + 1, first_rel[i] + 2 * j]],
                             axis=0) for j in range(NA_WIN_ROWS // 2)], axis=1)
        s = s + bias
        s_ref[u] = s
        row_max.append(jnp.max(s, axis=-1, keepdims=True))
    inv_sum = []
    for u in range(len(units)):
        e = jnp.exp2(s_ref[u] - row_max[u])
        inv_sum.append(1.0 / jnp.sum(e, axis=-1, keepdims=True))
        p_ref[u] = e.astype(BF16)
    for u, (i, hp) in enumerate(units):
        v2 = v_ref[0, pl.ds(starts[i], nk), hp * LANES:(hp + 1) * LANES]
        o2 = _dot(p_ref[u], v2) * inv_sum[u]
        o_ref[0, i * GRID_W:(i + 1) * GRID_W, hp * LANES:(hp + 1) * LANES] = (
            jnp.where(low, o2[:GRID_W], o2[GRID_W:]).astype(o_ref.dtype))


def _na_bias_kernel(rpb_ref, place_ref, mask_ref, o_ref):
    n_rel_cols = 2 * NA_WIN_COLS - 1
    n_rel_rows = 2 * NA_WIN_ROWS - 1
    base = pl.program_id(0) * (n_rel_rows * n_rel_cols)
    for r in range(n_rel_rows - 1):
        acc = mask_ref[...]
        for c in range(n_rel_cols):
            acc = acc + rpb_ref[base + r * n_rel_cols + c] * place_ref[c]
            acc = acc + rpb_ref[base + (r + 1) * n_rel_cols + c] * place_ref[n_rel_cols + c]
        o_ref[0, r] = acc


def _na_bias_tiles(rpb):
    n_heads, n_rel_rows, n_rel_cols = rpb.shape
    qc = np.arange(GRID_W)[:, None]
    kc = np.arange(GRID_W)[None, :]
    win_c0 = np.clip(qc - NA_WIN_COLS // 2, 0, GRID_W - NA_WIN_COLS)
    valid = (kc >= win_c0) & (kc < win_c0 + NA_WIN_COLS)
    place = np.zeros((2 * n_rel_cols, GRID_W, LANES), np.float32)
    for c in range(n_rel_cols):
        hit = ((kc - qc + NA_WIN_COLS - 1 == c) & valid).astype(np.float32)
        place[c, :, :GRID_W] = hit
        place[n_rel_cols + c, :, GRID_W:] = hit
    mask = np.tile(np.where(valid, 0.0, MASK_VALUE).astype(np.float32), (1, 2))
    return pl.pallas_call(
        _na_bias_kernel,
        grid=(n_heads,),
        in_specs=[pl.BlockSpec(memory_space=pltpu.SMEM), _const_spec(place.shape), _const_spec(mask.shape)],
        out_specs=pl.BlockSpec((1, n_rel_rows - 1, GRID_W, LANES), lambda hd: (hd, 0, 0, 0)),
        out_shape=jax.ShapeDtypeStruct((n_heads, n_rel_rows - 1, GRID_W, LANES), F32),
        compiler_params=_params("parallel"),
        name="na_bias",
    )((rpb.astype(F32) * LOG2E).reshape(-1), jnp.asarray(place), jnp.asarray(mask))


def _na_attn(q, k, v, rpb):
    bsz, seqlen, d = q.shape
    rows = seqlen // GRID_W
    assert rows >= NA_WIN_ROWS and rows % NA_ROWS_PER_STEP == 0
    bias = _na_bias_tiles(rpb)
    n_units = NA_ROWS_PER_STEP * (d // LANES)
    blk = NA_ROWS_PER_STEP * GRID_W
    return pl.pallas_call(
        functools.partial(_na_attn_kernel, rows=rows),
        grid=(bsz, rows // NA_ROWS_PER_STEP),
        in_specs=[pl.BlockSpec((1, blk, d), lambda bi, r: (bi, r, 0)),
                  pl.BlockSpec((1, seqlen, d), lambda bi, r: (bi, 0, 0)),
                  pl.BlockSpec((1, seqlen, d), lambda bi, r: (bi, 0, 0)),
                  _const_spec(bias.shape)],
        out_specs=pl.BlockSpec((1, blk, d), lambda bi, r: (bi, r, 0)),
        out_shape=jax.ShapeDtypeStruct((bsz, seqlen, d), BF16),
        scratch_shapes=[pltpu.VMEM((n_units, 2 * GRID_W, NA_WIN_ROWS * GRID_W), F32),
                        pltpu.VMEM((n_units, 2 * GRID_W, NA_WIN_ROWS * GRID_W), BF16)],
        compiler_params=_params("parallel", "arbitrary"),
        name="na_attn",
    )(q, k, v, bias)


def _na_mixer(h, bsz, mix_g, w_qkv_all, layer, q_norm, k_norm, rpb):
    t, d = h.shape
    seqlen = t // bsz
    q, k, v = _na_qkv(h, mix_g, w_qkv_all, layer, q_norm, k_norm)
    shp = (bsz, seqlen, d)
    return _na_attn(q.reshape(shp), k.reshape(shp), v.reshape(shp), rpb).reshape(t, d)


def kernel(x, p, ffn1_norm, ffn1_w_gu, ffn1_w_down, mix_norm, ffn2_norm, ffn2_w_gu, ffn2_w_down, ple_norm, ple_w_gate, ple_w_proj, ple_post_norm, ssd_w_in, ssd_conv_w, ssd_conv_b, ssd_dt_bias, ssd_a_log, ssd_d, ssd_norm, ssd_w_out, na_w_qkv, na_q_norm, na_k_norm, na_rpb, na_w_out):
    bsz, seqlen, d = x.shape
    depth = p.shape[0]
    t = bsz * seqlen
    h = x.reshape(t, d)
    p_all = p.reshape(depth, t, p.shape[-1])
    for i in range(depth):
        h = _ffn(h, ffn1_norm[i], ffn1_w_gu, ffn1_w_down, i)
        j = i // 2
        pre = None
        if i % 2 == 0:
            h = _ssd_mixer(h, bsz, mix_norm[i], ssd_w_in[j], ssd_conv_w[j], ssd_conv_b[j], ssd_dt_bias[j],
                           ssd_a_log[j], ssd_d[j], ssd_norm[j], ssd_w_out, j)
        else:
            pre = (_na_mixer(h, bsz, mix_norm[i], na_w_qkv, j, na_q_norm[j], na_k_norm[j], na_rpb[j]), na_w_out[j])
        h = _ffn(h, ffn2_norm[i], ffn2_w_gu, ffn2_w_down, i, pre=pre,
                 ple=(p_all, ple_norm[i], ple_w_gate[i], ple_w_proj[i], ple_post_norm[i]))
    return h.reshape(bsz, seqlen, d)
```
